```python
import math
import jax, jax.numpy as jnp
from jax import lax
import numpy as np

D_MODEL = 1024
BATCH = 16
SEQ = 4096
DEPTH = 1
DEC_BATCH = 8
DEC_SEQ = 4096
PAST_LEN = 128

N_META = 16
N_HEADS = 8
HEAD_DIM = 64
V_DIM = 2 * HEAD_DIM
QK_WIDTH = N_HEADS * 2 * HEAD_DIM
ATTN_WIDTH = N_HEADS * V_DIM
N_FOURIER_GROUPS = 4
FOURIER_GROUP_DIM = 128
FOURIER_WIDTH = N_FOURIER_GROUPS * FOURIER_GROUP_DIM
IN_COLS = 2 * QK_WIDTH + ATTN_WIDTH + FOURIER_WIDTH + 2 * D_MODEL
SPLITS = [QK_WIDTH, 2 * QK_WIDTH, 2 * QK_WIDTH + ATTN_WIDTH,
          2 * QK_WIDTH + ATTN_WIDTH + FOURIER_WIDTH,
          2 * QK_WIDTH + ATTN_WIDTH + FOURIER_WIDTH + D_MODEL]
NUM_BUCKETS = 32
MAX_DISTANCE = 128
Q_BLOCK = 128
N_EXPERTS = 32
TOP_K = 4
D_FF = D_MODEL
SWIGLU_LIMIT = 7.0
SWIGLU_ALPHA = 1.702
EXPERT_BLOCK = 256
NORM_EPS = 1e-6
SUBLN_EPS = 1e-5

kernel_name = 'hybrid_fnet_diffattn_moe_encoder'


def rms_norm(x, g, eps):
    xf = x.astype(jnp.float32)
    y = xf * lax.rsqrt(jnp.mean(xf * xf, axis=-1, keepdims=True) + eps)
    return (y * g.astype(jnp.float32)).astype(x.dtype)


def relative_bucket(rel):
    half = NUM_BUCKETS // 2
    max_exact = half // 2
    n = jnp.abs(rel)
    large = max_exact + (jnp.log(jnp.maximum(n, 1).astype(jnp.float32) / max_exact)
                         / math.log(MAX_DISTANCE / max_exact) * (half - max_exact)).astype(jnp.int32)
    large = jnp.minimum(large, half - 1)
    return jnp.where(rel > 0, half, 0) + jnp.where(n < max_exact, n, large)


def diff_attn_chunk(q, k, v, q_pos, rel_bias, lam):
    L = k.shape[1]
    k_pos = jnp.arange(L, dtype=jnp.int32)
    bucket = relative_bucket(k_pos[None, :] - q_pos[:, None])
    bias = jnp.transpose(rel_bias[bucket], (2, 0, 1)).astype(jnp.float32)
    s = jnp.einsum('bqhcd,bkhcd->bhcqk', q * (HEAD_DIM ** -0.5), k).astype(jnp.float32)
    p = jax.nn.softmax(s + bias[None, :, None], axis=-1)
    a = p[:, :, 0] - lam * p[:, :, 1]
    return jnp.einsum('bhqk,bkhe->bqhe', a.astype(v.dtype), v)


def diff_attention(q, k, v, rel_bias, lam):
    B, L = q.shape[0], q.shape[1]
    n_real = L - N_META
    n_blk = n_real // Q_BLOCK
    meta_out = diff_attn_chunk(q[:, :N_META], k, v, jnp.arange(N_META, dtype=jnp.int32), rel_bias, lam)
    q_real = q[:, N_META:].reshape(B, n_blk, Q_BLOCK, N_HEADS, 2, HEAD_DIM).swapaxes(0, 1)
    pos = (N_META + jnp.arange(n_real, dtype=jnp.int32)).reshape(n_blk, Q_BLOCK)
    real_out = lax.map(lambda qp: diff_attn_chunk(qp[0], k, v, qp[1], rel_bias, lam), (q_real, pos))
    real_out = real_out.swapaxes(0, 1).reshape(B, n_real, N_HEADS, V_DIM)
    return jnp.concatenate([meta_out, real_out], axis=1)


def fourier_mix(f):
    B, L = f.shape[0], f.shape[1]
    fg = f.reshape(B, L, N_FOURIER_GROUPS, FOURIER_GROUP_DIM).astype(jnp.float32)
    out = jnp.fft.fft2(fg, axes=(1, 3), norm='ortho').real
    return out.reshape(B, L, FOURIER_WIDTH).astype(f.dtype)


def moe(x, w_router, b_router, w_gate_up, b_gate_up, w_down, b_down):
    T, D = x.shape
    logits = (x @ w_router + b_router).astype(jnp.float32)
    top_val, top_idx = lax.top_k(logits, TOP_K)
    gate = jax.nn.softmax(top_val, axis=-1)
    A = T * TOP_K
    flat_e = top_idx.reshape(-1)
    order = jnp.argsort(flat_e)
    sorted_e = flat_e[order]
    tok_of = (order // TOP_K).astype(jnp.int32)
    counts = jnp.bincount(flat_e, length=N_EXPERTS)
    padded = ((counts + EXPERT_BLOCK - 1) // EXPERT_BLOCK) * EXPERT_BLOCK
    pad_end = jnp.cumsum(padded)
    pad_start = pad_end - padded
    grp_start = jnp.cumsum(counts) - counts
    dest = pad_start[sorted_e] + jnp.arange(A, dtype=jnp.int32) - grp_start[sorted_e]
    n_blocks = -(-A // EXPERT_BLOCK) + N_EXPERTS
    P = n_blocks * EXPERT_BLOCK
    row_src = jnp.full((P,), T, jnp.int32).at[dest].set(tok_of)
    row_w = jnp.zeros((P,), jnp.float32).at[dest].set(gate.reshape(-1)[order])
    block_e = jnp.minimum(jnp.searchsorted(pad_end, jnp.arange(n_blocks, dtype=jnp.int32) * EXPERT_BLOCK,
                                           side='right'), N_EXPERTS - 1)
    x_pad = jnp.concatenate([x, jnp.zeros((1, D), x.dtype)], axis=0)

    def expert_block(args):
        src, e = args
        h = x_pad[src] @ w_gate_up[e] + b_gate_up[e]
        g = jnp.minimum(h[:, ::2], SWIGLU_LIMIT)
        u = jnp.clip(h[:, 1::2], -SWIGLU_LIMIT, SWIGLU_LIMIT)
        act = (u + 1.0) * (g * jax.nn.sigmoid(g * SWIGLU_ALPHA))
        return act @ w_down[e] + b_down[e]

    y_buf = lax.map(expert_block, (row_src.reshape(n_blocks, EXPERT_BLOCK), block_e)).reshape(P, D)
    out = jnp.zeros((T + 1, D), x.dtype).at[row_src].add(y_buf * row_w[:, None].astype(x.dtype))
    return out[:T]


def encoder_layer(h, layer, g_mix, w_in, lambda_q1, lambda_k1, lambda_q2, lambda_k2, g_subln,
                  rel_bias, w_attn_branch, w_fourier_branch, w_out, g_ffn, w_router, b_router,
                  w_gate_up, b_gate_up, w_down, b_down):
    B, L, D = h.shape
    n = rms_norm(h, g_mix, NORM_EPS)
    q, k, v, f, ga, gf = jnp.split(n @ w_in, SPLITS, axis=-1)
    lam_init = 0.8 - 0.6 * math.exp(-0.3 * layer)
    lam = (jnp.exp(jnp.sum(lambda_q1.astype(jnp.float32) * lambda_k1.astype(jnp.float32)))
           - jnp.exp(jnp.sum(lambda_q2.astype(jnp.float32) * lambda_k2.astype(jnp.float32))) + lam_init)
    attn = diff_attention(q.reshape(B, L, N_HEADS, 2, HEAD_DIM), k.reshape(B, L, N_HEADS, 2, HEAD_DIM),
                          v.reshape(B, L, N_HEADS, V_DIM), rel_bias, lam)
    attn = (rms_norm(attn, g_subln, SUBLN_EPS) * (1.0 - lam_init)).reshape(B, L, ATTN_WIDTH)
    a_branch = attn @ w_attn_branch
    f_branch = fourier_mix(f) @ w_fourier_branch
    mixed = jax.nn.sigmoid(ga) * a_branch + jax.nn.sigmoid(gf) * f_branch
    h = h + mixed @ w_out
    n2 = rms_norm(h, g_ffn, NORM_EPS)
    return h + moe(n2.reshape(B * L, D), w_router, b_router, w_gate_up, b_gate_up,
                   w_down, b_down).reshape(B, L, D)


def encode(x, meta_tokens, g_mix, w_in, lambda_q1, lambda_k1, lambda_q2, lambda_k2, g_subln,
           rel_bias, w_attn_branch, w_fourier_branch, w_out, g_ffn, w_router, b_router,
           w_gate_up, b_gate_up, w_down, b_down, g_final):
    B = x.shape[0]
    meta = jnp.broadcast_to(meta_tokens[None].astype(x.dtype), (B, N_META, D_MODEL))
    h = jnp.concatenate([meta, x], axis=1)
    for layer in range(DEPTH):
        h = encoder_layer(h, layer, g_mix[layer], w_in[layer], lambda_q1[layer], lambda_k1[layer],
                          lambda_q2[layer], lambda_k2[layer], g_subln[layer], rel_bias,
                          w_attn_branch[layer], w_fourier_branch[layer], w_out[layer], g_ffn[layer],
                          w_router[layer], b_router[layer], w_gate_up[layer], b_gate_up[layer],
                          w_down[layer], b_down[layer])
    return rms_norm(h[:, N_META:], g_final, NORM_EPS)


def setup_inputs(seed: int = 0) -> dict:
    key = jax.random.key(seed)
    ks = jax.random.split(key, 24)
    f32 = jnp.float32

    def nrm(k, shape, scale):
        return jax.random.normal(k, shape, f32) * scale

    def gain(k, shape):
        return 1.0 + 0.02 * jax.random.normal(k, shape, f32)

    return {
        'x_prompt': nrm(ks[0], (BATCH, SEQ, D_MODEL), 1.0),
        'x_sample': nrm(ks[1], (DEC_BATCH, DEC_SEQ, D_MODEL), 1.0),
        'meta_tokens': nrm(ks[2], (N_META, D_MODEL), 1.0),
        'g_mix': gain(ks[3], (DEPTH, D_MODEL)),
        'w_in': nrm(ks[4], (DEPTH, D_MODEL, IN_COLS), D_MODEL ** -0.5),
        'lambda_q1': nrm(ks[5], (DEPTH, HEAD_DIM), 0.1),
        'lambda_k1': nrm(ks[6], (DEPTH, HEAD_DIM), 0.1),
        'lambda_q2': nrm(ks[7], (DEPTH, HEAD_DIM), 0.1),
        'lambda_k2': nrm(ks[8], (DEPTH, HEAD_DIM), 0.1),
        'g_subln': gain(ks[9], (DEPTH, V_DIM)),
        'rel_bias': nrm(ks[10], (NUM_BUCKETS, N_HEADS), 0.5),
        'w_attn_branch': nrm(ks[11], (DEPTH, ATTN_WIDTH, D_MODEL), ATTN_WIDTH ** -0.5),
        'w_fourier_branch': nrm(ks[12], (DEPTH, FOURIER_WIDTH, D_MODEL), FOURIER_WIDTH ** -0.5),
        'w_out': nrm(ks[13], (DEPTH, D_MODEL, D_MODEL), D_MODEL ** -0.5),
        'g_ffn': gain(ks[14], (DEPTH, D_MODEL)),
        'w_router': nrm(ks[15], (DEPTH, D_MODEL, N_EXPERTS), D_MODEL ** -0.5),
        'b_router': nrm(ks[16], (DEPTH, N_EXPERTS), 0.01),
        'w_gate_up': nrm(ks[17], (DEPTH, N_EXPERTS, D_MODEL, 2 * D_FF), D_MODEL ** -0.5),
        'b_gate_up': nrm(ks[18], (DEPTH, N_EXPERTS, 2 * D_FF), 0.01),
        'w_down': nrm(ks[19], (DEPTH, N_EXPERTS, D_FF, D_MODEL), D_FF ** -0.5),
        'b_down': nrm(ks[20], (DEPTH, N_EXPERTS, D_MODEL), 0.01),
        'g_final': gain(ks[21], (D_MODEL,)),
    }


def reference(x_prompt, x_sample, meta_tokens, g_mix, w_in, lambda_q1, lambda_k1, lambda_q2,
              lambda_k2, g_subln, rel_bias, w_attn_branch, w_fourier_branch, w_out, g_ffn,
              w_router, b_router, w_gate_up, b_gate_up, w_down, b_down, g_final):
    y_prompt = encode(x_prompt, meta_tokens, g_mix, w_in, lambda_q1, lambda_k1, lambda_q2, lambda_k2,
                      g_subln, rel_bias, w_attn_branch, w_fourier_branch, w_out, g_ffn, w_router,
                      b_router, w_gate_up, b_gate_up, w_down, b_down, g_final)
    y_sample = encode(x_sample, meta_tokens, g_mix, w_in, lambda_q1, lambda_k1, lambda_q2, lambda_k2,
                      g_subln, rel_bias, w_attn_branch, w_fourier_branch, w_out, g_ffn, w_router,
                      b_router, w_gate_up, b_gate_up, w_down, b_down, g_final)
    return (y_prompt, y_sample)
```

```python
import functools
import math

import jax
import jax.numpy as jnp
from jax import lax
from jax.experimental import pallas as pl
from jax.experimental.pallas import tpu as pltpu

F32 = jnp.float32
BF16 = jnp.bfloat16

D_MODEL = 1024
N_META = 16
N_HEADS = 8
HEAD_DIM = 64
V_DIM = 128
QK_WIDTH = N_HEADS * 2 * HEAD_DIM
ATTN_WIDTH = N_HEADS * V_DIM
N_GROUPS = 4
GROUP_DIM = 128
FOURIER_WIDTH = N_GROUPS * GROUP_DIM
IN_COLS = 2 * QK_WIDTH + ATTN_WIDTH + FOURIER_WIDTH + 2 * D_MODEL
NUM_BUCKETS = 32
MAX_DISTANCE = 128
N_EXPERTS = 32
TOP_K = 4
D_FF = D_MODEL
SWIGLU_LIMIT = 7.0
SWIGLU_ALPHA = 1.702
NORM_EPS = 1e-6
SUBLN_EPS = 1e-5
LAM_INIT = 0.8 - 0.6 * math.exp(-0.3 * 0)
FAR_REL = 91
NEG_BIG = -1e30
LANES = 128
VMEM_LIMIT = 56 * 1024 * 1024


def _cparams(sem):
    return pltpu.CompilerParams(dimension_semantics=sem, vmem_limit_bytes=VMEM_LIMIT)


def _in_proj_kernel(x_ref, g_ref, w_ref, dft_ref, q_ref, k_ref, v_ref, yc_ref, ys_ref,
                    sga_ref, sgf_ref):
    x = x_ref[...]
    ms = jnp.mean(x * x, axis=-1, keepdims=True)
    nb = ((x * lax.rsqrt(ms + NORM_EPS)) * g_ref[...]).astype(BF16)

    def proj(lo, hi):
        return jnp.dot(nb, w_ref[:, lo:hi], preferred_element_type=F32)

    c0 = QK_WIDTH
    c1 = 2 * QK_WIDTH
    c2 = c1 + ATTN_WIDTH
    c3 = c2 + FOURIER_WIDTH
    c4 = c3 + D_MODEL
    q_ref[...] = (proj(0, c0) * (HEAD_DIM ** -0.5)).astype(BF16)
    k_ref[...] = proj(c0, c1).astype(BF16)
    v_ref[...] = proj(c1, c2).astype(BF16)
    f = proj(c2, c3).astype(BF16)
    for g in range(N_GROUPS):
        lo, hi = g * GROUP_DIM, (g + 1) * GROUP_DIM
        y = jnp.dot(f[:, lo:hi], dft_ref[...], preferred_element_type=F32)
        yc_ref[:, lo:hi] = y[:, :GROUP_DIM].astype(BF16)
        ys_ref[:, lo:hi] = y[:, GROUP_DIM:].astype(BF16)
    sga_ref[...] = jax.nn.sigmoid(proj(c3, c4)).astype(BF16)
    sgf_ref[...] = jax.nn.sigmoid(proj(c4, IN_COLS)).astype(BF16)


def _in_proj(x2d, g_mix, w_in, dft_cs, tm):
    T = x2d.shape[0]
    row = lambda w: pl.BlockSpec((tm, w), lambda i: (i, 0))
    const = lambda a: pl.BlockSpec(a.shape, lambda i: (0,) * a.ndim)
    widths = (QK_WIDTH, QK_WIDTH, ATTN_WIDTH, FOURIER_WIDTH, FOURIER_WIDTH, D_MODEL, D_MODEL)
    return pl.pallas_call(
        _in_proj_kernel,
        grid=(T // tm,),
        in_specs=[row(D_MODEL), const(g_mix), const(w_in), const(dft_cs)],
        out_specs=[row(w) for w in widths],
        out_shape=[jax.ShapeDtypeStruct((T, w), BF16) for w in widths],
        compiler_params=_cparams(("parallel",)),
        name="in_proj",
    )(x2d, g_mix, w_in, dft_cs)


def _attn_kernel(lam_ref, q_ref, k_ref, v_ref, km_ref, vm_ref, bt_ref, bmeta_ref, g_ref, o_ref,
                 m_sc, l_sc, acc_sc, *, tq, tk, nkc, d_lo, d_hi):
    qi = pl.program_id(2)
    q = q_ref[...]
    col = lax.broadcasted_iota(jnp.int32, q.shape, 1)
    zero = jnp.zeros_like(q)
    qq = jnp.concatenate([jnp.where(col < HEAD_DIM, q, zero),
                          jnp.where(col >= HEAD_DIM, q, zero)], axis=0)
    nt = (((1,), (1,)), ((), ()))

    bmeta = bmeta_ref[jnp.minimum(qi, 1)]
    s = lax.dot_general(qq, km_ref[...], nt, preferred_element_type=F32)
    s = s + jnp.concatenate([bmeta, bmeta], axis=0)
    m0 = jnp.max(s, axis=-1, keepdims=True)
    p = jnp.exp(s - m0)
    m_sc[...] = jnp.broadcast_to(m0, m_sc.shape)
    l_sc[...] = jnp.broadcast_to(jnp.sum(p, axis=-1, keepdims=True), l_sc.shape)
    acc_sc[...] = jnp.dot(p.astype(BF16), vm_ref[...], preferred_element_type=F32)

    def body(c, carry):
        off = pl.multiple_of(c * tk, tk)
        kc = k_ref[pl.ds(off, tk), :]
        vc = v_ref[pl.ds(off, tk), :]
        s = lax.dot_general(qq, kc, nt, preferred_element_type=F32)
        bi = jnp.clip(c * (tk // tq) - qi, d_lo, d_hi) - d_lo
        bt = bt_ref[bi]
        s = s + jnp.concatenate([bt, bt], axis=0)
        m_prev = m_sc[...]
        m_next = jnp.maximum(m_prev, jnp.max(s, axis=-1, keepdims=True))
        alpha = jnp.exp(m_prev - m_next)
        p = jnp.exp(s - m_next[:, :1])
        l_sc[...] = alpha * l_sc[...] + jnp.sum(p, axis=-1, keepdims=True)
        acc_sc[...] = alpha * acc_sc[...] + jnp.dot(p.astype(BF16), vc,
                                                    preferred_element_type=F32)
        m_sc[...] = m_next
        return carry

    lax.fori_loop(0, nkc, body, 0)
    o = acc_sc[...] / l_sc[...]
    o = o[:tq] - lam_ref[0] * o[tq:]
    ms = jnp.mean(o * o, axis=-1, keepdims=True)
    o = (o * lax.rsqrt(ms + SUBLN_EPS)) * g_ref[...] * (1.0 - LAM_INIT)
    o_ref[...] = o.astype(BF16)


def _bias_range(tq, tk):
    d_lo = -((tk - 1 + FAR_REL + tq - 1) // tq)
    d_hi = (FAR_REL + tq - 1 + tq - 1) // tq
    return d_lo, d_hi


def _relative_bucket(rel):
    half = NUM_BUCKETS // 2
    max_exact = half // 2
    n = jnp.abs(rel)
    large = max_exact + (jnp.log(jnp.maximum(n, 1).astype(F32) / max_exact)
                         / math.log(MAX_DISTANCE / max_exact) * (half - max_exact)).astype(jnp.int32)
    large = jnp.minimum(large, half - 1)
    return jnp.where(rel > 0, half, 0) + jnp.where(n < max_exact, n, large)


def _bias_tables(rel_bias, tq, tk):
    d_lo, d_hi = _bias_range(tq, tk)
    r = jnp.arange(tq, dtype=jnp.int32)[None, :, None]
    j = jnp.arange(tk, dtype=jnp.int32)[None, None, :]
    d = jnp.arange(d_lo, d_hi + 1, dtype=jnp.int32)[:, None, None]
    tab = rel_bias[_relative_bucket(j - r + d * tq)].astype(F32)
    tab = jnp.transpose(tab, (3, 0, 1, 2))
    jm = jnp.arange(LANES, dtype=jnp.int32)[None, None, :]
    qb = jnp.arange(2, dtype=jnp.int32)[:, None, None]
    relm = jm - (N_META + qb * tq + r)
    tabm = rel_bias[_relative_bucket(relm)].astype(F32)
    tabm = jnp.where((jm < N_META)[..., None], tabm, NEG_BIG)
    tabm = jnp.transpose(tabm, (3, 0, 1, 2))
    return tab, tabm, d_lo, d_hi


def _attention(lam, q, k, v, km, vm, rel_bias, g_subln, B, S, tq, tk):
    T = B * S
    nq = S // tq
    tab, tabm, d_lo, d_hi = _bias_tables(rel_bias, tq, tk)
    nd = d_hi - d_lo + 1
    kern = functools.partial(_attn_kernel, tq=tq, tk=tk, nkc=S // tk, d_lo=d_lo, d_hi=d_hi)
    return pl.pallas_call(
        kern,
        grid=(B, N_HEADS, nq),
        in_specs=[
            pl.BlockSpec(memory_space=pltpu.SMEM),
            pl.BlockSpec((tq, V_DIM), lambda b, h, i: (b * nq + i, h)),
            pl.BlockSpec((S, V_DIM), lambda b, h, i: (b, h)),
            pl.BlockSpec((S, V_DIM), lambda b, h, i: (b, h)),
            pl.BlockSpec((LANES, V_DIM), lambda b, h, i: (0, h)),
            pl.BlockSpec((LANES, V_DIM), lambda b, h, i: (0, h)),
            pl.BlockSpec((None, nd, tq, tk), lambda b, h, i: (h, 0, 0, 0)),
            pl.BlockSpec((None, 2, tq, LANES), lambda b, h, i: (h, 0, 0, 0)),
            pl.BlockSpec((1, V_DIM), lambda b, h, i: (0, 0)),
        ],
        out_specs=pl.BlockSpec((tq, V_DIM), lambda b, h, i: (b * nq + i, h)),
        out_shape=jax.ShapeDtypeStruct((T, ATTN_WIDTH), BF16),
        scratch_shapes=[pltpu.VMEM((2 * tq, LANES), F32),
                        pltpu.VMEM((2 * tq, LANES), F32),
                        pltpu.VMEM((2 * tq, V_DIM), F32)],
        compiler_params=_cparams(("parallel", "parallel", "arbitrary")),
        name="attn",
    )(lam, q, k, v, km, vm, tab, tabm, g_subln)


def _seq_dft_kernel(c_ref, s_ref, cm_ref, ym_ref, yc_ref, ys_ref, o_ref, acc_sc, *, tn):
    nt = pl.program_id(2)

    @pl.when(nt == 0)
    def _():
        acc_sc[...] = jnp.dot(cm_ref[...], ym_ref[...], preferred_element_type=F32)

    off = pl.multiple_of(nt * tn, tn)
    acc_sc[...] += (jnp.dot(c_ref[...], yc_ref[pl.ds(off, tn), :], preferred_element_type=F32)
                    + jnp.dot(s_ref[...], ys_ref[pl.ds(off, tn), :], preferred_element_type=F32))

    @pl.when(nt == pl.num_programs(2) - 1)
    def _():
        o_ref[...] = acc_sc[...].astype(BF16)


def _dft_matrices(S):
    L = S + N_META
    scale = 1.0 / math.sqrt(L * GROUP_DIM)
    pos = N_META + jnp.arange(S, dtype=jnp.int32)
    ang = ((pos[:, None] * pos[None, :]) % L).astype(F32) * (2.0 * math.pi / L)
    cmat = (jnp.cos(ang) * scale).astype(BF16)
    smat = (jnp.sin(ang) * (-scale)).astype(BF16)
    angm = ((pos[:, None] * jnp.arange(N_META, dtype=jnp.int32)[None, :]) % L).astype(F32) \
        * (2.0 * math.pi / L)
    cm = jnp.concatenate([jnp.cos(angm) * scale, jnp.sin(angm) * (-scale)], axis=1).astype(BF16)
    return cmat, smat, cm


def _seq_dft(yc, ys, ymeta, B, S, tmo, tn):
    T = B * S
    nk = S // tmo
    cmat, smat, cm = _dft_matrices(S)
    return pl.pallas_call(
        functools.partial(_seq_dft_kernel, tn=tn),
        grid=(B, nk, S // tn),
        in_specs=[
            pl.BlockSpec((tmo, tn), lambda b, kt, nt: (kt, nt)),
            pl.BlockSpec((tmo, tn), lambda b, kt, nt: (kt, nt)),
            pl.BlockSpec((tmo, 2 * N_META), lambda b, kt, nt: (kt, 0)),
            pl.BlockSpec((2 * N_META, FOURIER_WIDTH), lambda b, kt, nt: (0, 0)),
            pl.BlockSpec((S, FOURIER_WIDTH), lambda b, kt, nt: (b, 0)),
            pl.BlockSpec((S, FOURIER_WIDTH), lambda b, kt, nt: (b, 0)),
        ],
        out_specs=pl.BlockSpec((tmo, FOURIER_WIDTH), lambda b, kt, nt: (b * nk + kt, 0)),
        out_shape=jax.ShapeDtypeStruct((T, FOURIER_WIDTH), BF16),
        scratch_shapes=[pltpu.VMEM((tmo, FOURIER_WIDTH), F32)],
        compiler_params=_cparams(("parallel", "parallel", "arbitrary")),
        name="seq_dft",
    )(cmat, smat, cm, ymeta, yc, ys)


def _merge_kernel(attn_ref, four_ref, sga_ref, sgf_ref, x_ref, wab_ref, wfb_ref, wout_ref,
                  gffn_ref, wrh_ref, wrl_ref, br_ref, h_ref, n2_ref, gate_ref, idx_ref):
    a = jnp.dot(attn_ref[...], wab_ref[...], preferred_element_type=F32)
    fb = jnp.dot(four_ref[...], wfb_ref[...], preferred_element_type=F32)
    mixed = sga_ref[...].astype(F32) * a + sgf_ref[...].astype(F32) * fb
    h = x_ref[...] + jnp.dot(mixed.astype(BF16), wout_ref[...], preferred_element_type=F32)
    h_ref[...] = h
    ms = jnp.mean(h * h, axis=-1, keepdims=True)
    n2 = (h * lax.rsqrt(ms + NORM_EPS)) * gffn_ref[...]
    n2_ref[...] = n2
    nh = n2.astype(BF16)
    nl = (n2 - nh.astype(F32)).astype(BF16)
    lg = (jnp.dot(nh, wrh_ref[...], preferred_element_type=F32)
          + jnp.dot(nl, wrh_ref[...], preferred_element_type=F32)
          + jnp.dot(nh, wrl_ref[...], preferred_element_type=F32)) + br_ref[...]
    col = lax.broadcasted_iota(jnp.int32, lg.shape, 1)
    colf = col.astype(F32)
    vals = []
    idx_out = jnp.zeros(lg.shape, jnp.int32)
    for kk in range(TOP_K):
        mx = jnp.max(lg, axis=-1, keepdims=True)
        am = jnp.min(jnp.where(lg == mx, colf, float(LANES)), axis=-1, keepdims=True)
        am = am.astype(jnp.int32)
        vals.append(mx)
        idx_out = jnp.where(col == kk, am, idx_out)
        lg = jnp.where(col == am, -jnp.inf, lg)
    es = [jnp.exp(v - vals[0]) for v in vals]
    den = es[0] + es[1] + es[2] + es[3]
    gate_out = jnp.zeros(lg.shape, F32)
    for kk in range(TOP_K):
        gate_out = jnp.where(col == kk, es[kk] / den, gate_out)
    gate_ref[...] = gate_out
    idx_ref[...] = idx_out


def _merge(attn, four, sga, sgf, x2d, wab, wfb, wout, gffn, wrh, wrl, br, tm):
    T = x2d.shape[0]
    row = lambda w: pl.BlockSpec((tm, w), lambda i: (i, 0))
    const = lambda a: pl.BlockSpec(a.shape, lambda i: (0,) * a.ndim)
    return pl.pallas_call(
        _merge_kernel,
        grid=(T // tm,),
        in_specs=[row(ATTN_WIDTH), row(FOURIER_WIDTH), row(D_MODEL), row(D_MODEL), row(D_MODEL),
                  const(wab), const(wfb), const(wout), const(gffn), const(wrh), const(wrl),
                  const(br)],
        out_specs=[row(D_MODEL), row(D_MODEL), row(LANES), row(LANES)],
        out_shape=[jax.ShapeDtypeStruct((T, D_MODEL), F32),
                   jax.ShapeDtypeStruct((T, D_MODEL), F32),
                   jax.ShapeDtypeStruct((T, LANES), F32),
                   jax.ShapeDtypeStruct((T, LANES), jnp.int32)],
        compiler_params=_cparams(("parallel",)),
        name="merge",
    )(attn, four, sga, sgf, x2d, wab, wfb, wout, gffn, wrh, wrl, br)


def _row_copy(src_ref, src_row, dst_ref, dst_row, sem):
    return pltpu.make_async_copy(src_ref.at[pl.ds(src_row, 1), :],
                                 dst_ref.at[pl.ds(dst_row, 1), :], sem)


def _moe_kernel(be_ref, nu_ref, src_hbm, dst_hbm, x_hbm, wg_ref, wu_ref, bg_ref, bu_ref, wd_ref,
                bd_ref, y_hbm, src_s, dst_s, xbuf, ybuf, sem, *, bm, unroll):
    del be_ref
    i = pl.program_id(0)

    @pl.when(i < nu_ref[0])
    def _():
        c_src = pltpu.make_async_copy(src_hbm.at[i], src_s, sem.at[2])
        c_dst = pltpu.make_async_copy(dst_hbm.at[i], dst_s, sem.at[3])
        c_src.start()
        c_dst.start()
        c_src.wait()
        c_dst.wait()

        def gather_start(r, c):
            _row_copy(x_hbm, src_s[0, r], xbuf, r, sem.at[0]).start()
            return c

        def gather_wait(r, c):
            _row_copy(x_hbm, 0, xbuf, r, sem.at[0]).wait()
            return c

        lax.fori_loop(0, bm, gather_start, 0, unroll=unroll)
        lax.fori_loop(0, bm, gather_wait, 0, unroll=unroll)

        x = xbuf[...].astype(BF16)
        g = jnp.dot(x, wg_ref[...], preferred_element_type=F32) + bg_ref[...]
        u = jnp.dot(x, wu_ref[...], preferred_element_type=F32) + bu_ref[...]
        g = jnp.minimum(g, SWIGLU_LIMIT)
        u = jnp.clip(u, -SWIGLU_LIMIT, SWIGLU_LIMIT)
        act = (u + 1.0) * (g * jax.nn.sigmoid(g * SWIGLU_ALPHA))
        ybuf[...] = jnp.dot(act.astype(BF16), wd_ref[...], preferred_element_type=F32) + bd_ref[...]

        def scatter_start(r, c):
            a = dst_s[0, r]

            @pl.when(a >= 0)
            def _():
                _row_copy(ybuf, r, y_hbm, a, sem.at[1]).start()
            return c

        def scatter_wait(r, c):
            @pl.when(dst_s[0, r] >= 0)
            def _():
                _row_copy(ybuf, r, y_hbm, 0, sem.at[1]).wait()
            return c

        lax.fori_loop(0, bm, scatter_start, 0, unroll=unroll)
        lax.fori_loop(0, bm, scatter_wait, 0, unroll=unroll)


def _moe(block_e, n_used, row_src, row_dst, n2, wg, wu, bg, bu, wd, bd, n_rows_out, bm):
    n_blocks = row_src.shape[0]
    wspec = pl.BlockSpec((None, D_MODEL, D_FF), lambda i, be, nu: (be[i], 0, 0))
    bspec = pl.BlockSpec((None, 1, D_FF), lambda i, be, nu: (be[i], 0, 0))
    anyspec = pl.BlockSpec(memory_space=pl.ANY)
    return pl.pallas_call(
        functools.partial(_moe_kernel, bm=bm, unroll=8),
        grid_spec=pltpu.PrefetchScalarGridSpec(
            num_scalar_prefetch=2,
            grid=(n_blocks,),
            in_specs=[anyspec, anyspec, anyspec, wspec, wspec, bspec, bspec, wspec, bspec],
            out_specs=anyspec,
            scratch_shapes=[pltpu.SMEM((1, bm), jnp.int32),
                            pltpu.SMEM((1, bm), jnp.int32),
                            pltpu.VMEM((bm, D_MODEL), F32),
                            pltpu.VMEM((bm, D_MODEL), F32),
                            pltpu.SemaphoreType.DMA((4,))]),
        out_shape=jax.ShapeDtypeStruct((n_rows_out, D_MODEL), F32),
        compiler_params=_cparams(("arbitrary",)),
        name="moe",
    )(block_e, n_used, row_src, row_dst, n2, wg, wu, bg, bu, wd, bd)


def _routing(top_idx, T, bm):
    A = T * TOP_K
    flat_e = top_idx.reshape(-1)
    order = jnp.argsort(flat_e, stable=True).astype(jnp.int32)
    counts = jnp.sum((flat_e[:, None] == jnp.arange(N_EXPERTS, dtype=jnp.int32)[None, :])
                     .astype(jnp.int32), axis=0)
    nblk_e = (counts + bm - 1) // bm
    blk_end = jnp.cumsum(nblk_e)
    blk_start = blk_end - nblk_e
    grp_start = jnp.cumsum(counts) - counts
    n_blocks = -(-A // bm) + N_EXPERTS
    blk = jnp.arange(n_blocks, dtype=jnp.int32)
    block_e = jnp.minimum(jnp.searchsorted(blk_end, blk, side='right'), N_EXPERTS - 1).astype(jnp.int32)
    offs = (blk - blk_start[block_e])[:, None] * bm + jnp.arange(bm, dtype=jnp.int32)[None, :]
    valid = offs < counts[block_e][:, None]
    a = order[jnp.clip(grp_start[block_e][:, None] + offs, 0, A - 1)]
    row_src = jnp.where(valid, a // TOP_K, 0).astype(jnp.int32)
    row_dst = jnp.where(valid, (a % TOP_K) * T + a // TOP_K, -1).astype(jnp.int32)
    n_used = blk_end[-1:].astype(jnp.int32)
    return block_e, n_used, row_src[:, None, :], row_dst[:, None, :]


def _final_kernel(h_ref, y_ref, gate_ref, g_ref, o_ref):
    acc = h_ref[...]
    gate = gate_ref[...]
    for kk in range(TOP_K):
        acc = acc + gate[:, kk:kk + 1] * y_ref[kk]
    ms = jnp.mean(acc * acc, axis=-1, keepdims=True)
    o_ref[...] = (acc * lax.rsqrt(ms + NORM_EPS)) * g_ref[...]


def _final(h, y3, gate, g_final, tm):
    T = h.shape[0]
    return pl.pallas_call(
        _final_kernel,
        grid=(T // tm,),
        in_specs=[pl.BlockSpec((tm, D_MODEL), lambda i: (i, 0)),
                  pl.BlockSpec((TOP_K, tm, D_MODEL), lambda i: (0, i, 0)),
                  pl.BlockSpec((tm, LANES), lambda i: (i, 0)),
                  pl.BlockSpec((1, D_MODEL), lambda i: (0, 0))],
        out_specs=pl.BlockSpec((tm, D_MODEL), lambda i: (i, 0)),
        out_shape=jax.ShapeDtypeStruct((T, D_MODEL), F32),
        compiler_params=_cparams(("parallel",)),
        name="final",
    )(h, y3, gate, g_final)


def _encode_tokens(x, meta_tokens, g_mix, w_in, lambda_q1, lambda_k1, lambda_q2, lambda_k2,
                   g_subln, rel_bias, w_attn_branch, w_fourier_branch, w_out, g_ffn, w_router,
                   b_router, w_gate_up, b_gate_up, w_down, b_down, g_final):
    B, S, D = x.shape
    T = B * S
    tm = min(512, T)
    tq = min(256, S)
    tk = min(512, S)
    tdft = min(1024, S)
    bm = 256

    x2d = x.reshape(T, D)
    gm = g_mix[0].reshape(1, D)
    w_in_b = w_in[0].astype(BF16)
    cc = jnp.arange(GROUP_DIM, dtype=jnp.int32)
    angc = ((cc[:, None] * cc[None, :]) % GROUP_DIM).astype(F32) * (2.0 * math.pi / GROUP_DIM)
    dft_cs = jnp.concatenate([jnp.cos(angc), jnp.sin(angc)], axis=1).astype(BF16)

    q, k, v, yc, ys, sga, sgf = _in_proj(x2d, gm, w_in_b, dft_cs, tm)
    _, k_m, v_m, yc_m, ys_m, _, _ = _in_proj(meta_tokens.astype(F32), gm, w_in_b, dft_cs, N_META)

    lam = (jnp.exp(jnp.sum(lambda_q1[0].astype(F32) * lambda_k1[0].astype(F32)))
           - jnp.exp(jnp.sum(lambda_q2[0].astype(F32) * lambda_k2[0].astype(F32))) + LAM_INIT)
    pad = ((0, LANES - N_META), (0, 0))
    attn = _attention(lam.reshape(1), q, k, v, jnp.pad(k_m, pad), jnp.pad(v_m, pad), rel_bias,
                      g_subln[0].reshape(1, V_DIM), B, S, tq, tk)

    ymeta = jnp.concatenate([yc_m, ys_m], axis=0)
    four = _seq_dft(yc, ys, ymeta, B, S, tdft, tdft)

    wr = jnp.pad(w_router[0], ((0, 0), (0, LANES - N_EXPERTS)))
    wrh = wr.astype(BF16)
    wrl = (wr - wrh.astype(F32)).astype(BF16)
    br = jnp.pad(b_router[0], (0, LANES - N_EXPERTS), constant_values=NEG_BIG).reshape(1, LANES)
    h, n2, gate, idx = _merge(attn, four, sga, sgf, x2d, w_attn_branch[0].astype(BF16),
                              w_fourier_branch[0].astype(BF16), w_out[0].astype(BF16),
                              g_ffn[0].reshape(1, D), wrh, wrl, br, tm)

    block_e, n_used, row_src, row_dst = _routing(idx[:, :TOP_K], T, bm)
    wg = w_gate_up[0][:, :, 0::2].astype(BF16)
    wu = w_gate_up[0][:, :, 1::2].astype(BF16)
    bg = b_gate_up[0][:, None, 0::2]
    bu = b_gate_up[0][:, None, 1::2]
    y = _moe(block_e, n_used, row_src, row_dst, n2, wg, wu, bg, bu, w_down[0].astype(BF16),
             b_down[0][:, None, :], TOP_K * T, bm)

    out = _final(h, y.reshape(TOP_K, T, D), gate, g_final.reshape(1, D), min(256, T))
    return out.reshape(B, S, D)


def kernel(x_prompt, x_sample, meta_tokens, g_mix, w_in, lambda_q1, lambda_k1, lambda_q2, lambda_k2, g_subln, rel_bias, w_attn_branch, w_fourier_branch, w_out, g_ffn, w_router, b_router, w_gate_up, b_gate_up, w_down, b_down, g_final):
    nb = x_prompt.shape[0]
    x = jnp.concatenate([x_prompt, x_sample], axis=0)
    out = _encode_tokens(x, meta_tokens, g_mix, w_in, lambda_q1, lambda_k1, lambda_q2, lambda_k2,
                         g_subln, rel_bias, w_attn_branch, w_fourier_branch, w_out, g_ffn,
                         w_router, b_router, w_gate_up, b_gate_up, w_down, b_down, g_final)
    return out[:nb], out[nb:]
```

```python
import functools
import math

import jax
import jax.numpy as jnp
from jax import lax
from jax.experimental import pallas as pl
from jax.experimental.pallas import tpu as pltpu

F32 = jnp.float32
BF16 = jnp.bfloat16

D_MODEL = 1024
N_META = 16
N_HEADS = 8
HEAD_DIM = 64
V_DIM = 128
QK_WIDTH = N_HEADS * 2 * HEAD_DIM
ATTN_WIDTH = N_HEADS * V_DIM
N_GROUPS = 4
GROUP_DIM = 128
FOURIER_WIDTH = N_GROUPS * GROUP_DIM
IN_COLS = 2 * QK_WIDTH + ATTN_WIDTH + FOURIER_WIDTH + 2 * D_MODEL
NUM_BUCKETS = 32
MAX_DISTANCE = 128
N_EXPERTS = 32
TOP_K = 4
D_FF = D_MODEL
SWIGLU_LIMIT = 7.0
SWIGLU_ALPHA = 1.702
NORM_EPS = 1e-6
SUBLN_EPS = 1e-5
LAM_INIT = 0.8 - 0.6 * math.exp(-0.3 * 0)
FAR_REL = 91
NEG_BIG = -1e30
LOG2E = math.log2(math.e)
LANES = 128
ATTN_TILES = (512, 256, 512)
VMEM_LIMIT = 56 * 1024 * 1024


def _cparams(sem):
    return pltpu.CompilerParams(dimension_semantics=sem, vmem_limit_bytes=VMEM_LIMIT)


def _in_proj_kernel(x_ref, g_ref, w_ref, dft_ref, q_ref, k_ref, v_ref, yc_ref, ys_ref,
                    sga_ref, sgf_ref):
    x = x_ref[...]
    ms = jnp.mean(x * x, axis=-1, keepdims=True)
    nb = ((x * lax.rsqrt(ms + NORM_EPS)) * g_ref[...]).astype(BF16)

    def proj(lo, hi):
        return jnp.dot(nb, w_ref[:, lo:hi], preferred_element_type=F32)

    c0 = QK_WIDTH
    c1 = 2 * QK_WIDTH
    c2 = c1 + ATTN_WIDTH
    c3 = c2 + FOURIER_WIDTH
    c4 = c3 + D_MODEL
    q_ref[...] = (proj(0, c0) * (HEAD_DIM ** -0.5 * LOG2E)).astype(BF16)
    k_ref[...] = proj(c0, c1).astype(BF16)
    v_ref[...] = proj(c1, c2).astype(BF16)
    f = proj(c2, c3).astype(BF16)
    for g in range(N_GROUPS):
        lo, hi = g * GROUP_DIM, (g + 1) * GROUP_DIM
        y = jnp.dot(f[:, lo:hi], dft_ref[...], preferred_element_type=F32)
        yc_ref[:, lo:hi] = y[:, :GROUP_DIM].astype(BF16)
        ys_ref[:, lo:hi] = y[:, GROUP_DIM:].astype(BF16)
    sga_ref[...] = jax.nn.sigmoid(proj(c3, c4)).astype(BF16)
    sgf_ref[...] = jax.nn.sigmoid(proj(c4, IN_COLS)).astype(BF16)


def _in_proj(x2d, g_mix, w_in, dft_cs, tm):
    T = x2d.shape[0]
    row = lambda w: pl.BlockSpec((tm, w), lambda i: (i, 0))
    const = lambda a: pl.BlockSpec(a.shape, lambda i: (0,) * a.ndim)
    widths = (QK_WIDTH, QK_WIDTH, ATTN_WIDTH, FOURIER_WIDTH, FOURIER_WIDTH, D_MODEL, D_MODEL)
    return pl.pallas_call(
        _in_proj_kernel,
        grid=(T // tm,),
        in_specs=[row(D_MODEL), const(g_mix), const(w_in), const(dft_cs)],
        out_specs=[row(w) for w in widths],
        out_shape=[jax.ShapeDtypeStruct((T, w), BF16) for w in widths],
        compiler_params=_cparams(("arbitrary",)),
        name="in_proj",
    )(x2d, g_mix, w_in, dft_cs)


def _attn_kernel(lam_ref, q_ref, k_ref, v_ref, km_ref, vm_ref, bt_ref, bmeta_ref, g_ref, o_ref,
                 m_sc, l_sc, acc_sc, *, tq, rq, tk, nkc, d_lo, d_hi):
    qi = pl.program_id(2)
    nsub = tq // rq
    nt = (((1,), (1,)), ((), ()))
    col = lax.broadcasted_iota(jnp.int32, (rq, V_DIM), 1)
    chains = []
    for j in range(nsub):
        q = q_ref[j * rq:(j + 1) * rq, :]
        zero = jnp.zeros_like(q)
        chains.append((j * rq, jnp.where(col < HEAD_DIM, q, zero), j))
        chains.append((tq + j * rq, jnp.where(col >= HEAD_DIM, q, zero), j))

    km = km_ref[...]
    vm = vm_ref[...]
    for row0, qc, j in chains:
        rows = pl.ds(row0, rq)
        bmeta = bmeta_ref[jnp.minimum(qi * nsub + j, 1)]
        s = lax.dot_general(qc, km, nt, preferred_element_type=F32) + bmeta
        m0 = jnp.max(s, axis=-1, keepdims=True)
        p = jnp.exp2(s - m0)
        m_sc[rows, :] = jnp.broadcast_to(m0, (rq, LANES))
        l_sc[rows, :] = p
        acc_sc[rows, :] = jnp.dot(p.astype(BF16), vm, preferred_element_type=F32)

    def body(c, carry):
        off = pl.multiple_of(c * tk, tk)
        kc = k_ref[pl.ds(off, tk), :]
        vc = v_ref[pl.ds(off, tk), :]
        for row0, qc, j in chains:
            rows = pl.ds(row0, rq)
            bi = jnp.clip(c * (tk // rq) - (qi * nsub + j), d_lo, d_hi) - d_lo
            s = lax.dot_general(qc, kc, nt, preferred_element_type=F32) + bt_ref[bi]
            m_prev = m_sc[rows, :]
            m_next = jnp.maximum(m_prev, jnp.max(s, axis=-1, keepdims=True))
            alpha = jnp.exp2(m_prev - m_next)
            p = jnp.exp2(s - jnp.tile(m_next, (1, tk // LANES)))
            psum = p[:, 0:LANES]
            for t in range(1, tk // LANES):
                psum = psum + p[:, t * LANES:(t + 1) * LANES]
            l_sc[rows, :] = alpha * l_sc[rows, :] + psum
            acc_sc[rows, :] = alpha * acc_sc[rows, :] + jnp.dot(p.astype(BF16), vc,
                                                                preferred_element_type=F32)
            m_sc[rows, :] = m_next
        return carry

    lax.fori_loop(0, nkc, body, 0)
    l = jnp.sum(l_sc[...], axis=-1, keepdims=True)
    o = acc_sc[...] / l
    o = o[:tq] - lam_ref[0] * o[tq:]
    ms = jnp.mean(o * o, axis=-1, keepdims=True)
    o = (o * lax.rsqrt(ms + SUBLN_EPS)) * g_ref[...] * (1.0 - LAM_INIT)
    o_ref[...] = o.astype(BF16)


def _bias_range(tq, tk):
    d_lo = -((tk - 1 + FAR_REL + tq - 1) // tq)
    d_hi = (FAR_REL + tq - 1 + tq - 1) // tq
    return d_lo, d_hi


def _relative_bucket(rel):
    half = NUM_BUCKETS // 2
    max_exact = half // 2
    n = jnp.abs(rel)
    large = max_exact + (jnp.log(jnp.maximum(n, 1).astype(F32) / max_exact)
                         / math.log(MAX_DISTANCE / max_exact) * (half - max_exact)).astype(jnp.int32)
    large = jnp.minimum(large, half - 1)
    return jnp.where(rel > 0, half, 0) + jnp.where(n < max_exact, n, large)


def _bias_tables(rel_bias, rq, tk):
    d_lo, d_hi = _bias_range(rq, tk)
    rb = rel_bias.astype(F32) * LOG2E

    def lookup(rel):
        onehot = (_relative_bucket(rel)[..., None]
                  == jnp.arange(NUM_BUCKETS, dtype=jnp.int32)).astype(F32)
        return jnp.einsum('xyzb,bh->hxyz', onehot, rb, precision=lax.Precision.HIGHEST)

    r = jnp.arange(rq, dtype=jnp.int32)[None, :, None]
    j = jnp.arange(tk, dtype=jnp.int32)[None, None, :]
    d = jnp.arange(d_lo, d_hi + 1, dtype=jnp.int32)[:, None, None]
    tab = lookup(j - r + d * rq)
    jm = jnp.arange(LANES, dtype=jnp.int32)[None, None, :]
    sb = jnp.arange(2, dtype=jnp.int32)[:, None, None]
    tabm = jnp.where(jm < N_META, lookup(jm - (N_META + sb * rq + r)), NEG_BIG)
    return tab, tabm, d_lo, d_hi


def _attention(lam, q, k, v, km, vm, rel_bias, g_subln, B, S, tq, rq, tk):
    T = B * S
    nq = S // tq
    tab, tabm, d_lo, d_hi = _bias_tables(rel_bias, rq, tk)
    nd = d_hi - d_lo + 1
    kern = functools.partial(_attn_kernel, tq=tq, rq=rq, tk=tk, nkc=S // tk, d_lo=d_lo, d_hi=d_hi)
    return pl.pallas_call(
        kern,
        grid=(B, N_HEADS, nq),
        in_specs=[
            pl.BlockSpec(memory_space=pltpu.SMEM),
            pl.BlockSpec((tq, V_DIM), lambda b, h, i: (b * nq + i, h)),
            pl.BlockSpec((S, V_DIM), lambda b, h, i: (b, h)),
            pl.BlockSpec((S, V_DIM), lambda b, h, i: (b, h)),
            pl.BlockSpec((LANES, V_DIM), lambda b, h, i: (0, h)),
            pl.BlockSpec((LANES, V_DIM), lambda b, h, i: (0, h)),
            pl.BlockSpec((None, nd, rq, tk), lambda b, h, i: (h, 0, 0, 0)),
            pl.BlockSpec((None, 2, rq, LANES), lambda b, h, i: (h, 0, 0, 0)),
            pl.BlockSpec((1, V_DIM), lambda b, h, i: (0, 0)),
        ],
        out_specs=pl.BlockSpec((tq, V_DIM), lambda b, h, i: (b * nq + i, h)),
        out_shape=jax.ShapeDtypeStruct((T, ATTN_WIDTH), BF16),
        scratch_shapes=[pltpu.VMEM((2 * tq, LANES), F32),
                        pltpu.VMEM((2 * tq, LANES), F32),
                        pltpu.VMEM((2 * tq, V_DIM), F32)],
        compiler_params=_cparams(("arbitrary", "arbitrary", "arbitrary")),
        name="attn",
    )(lam, q, k, v, km, vm, tab, tabm, g_subln)


def _seq_dft_kernel(c_ref, s_ref, cm_ref, ym_ref, yc_ref, ys_ref, o_ref, acc_sc, *, tn):
    nt = pl.program_id(2)

    @pl.when(nt == 0)
    def _():
        acc_sc[...] = jnp.dot(cm_ref[...], ym_ref[...], preferred_element_type=F32)

    off = pl.multiple_of(nt * tn, tn)
    acc_sc[...] += (jnp.dot(c_ref[...], yc_ref[pl.ds(off, tn), :], preferred_element_type=F32)
                    + jnp.dot(s_ref[...], ys_ref[pl.ds(off, tn), :], preferred_element_type=F32))

    @pl.when(nt == pl.num_programs(2) - 1)
    def _():
        o_ref[...] = acc_sc[...].astype(BF16)


def _dft_matrices(S):
    L = S + N_META
    scale = 1.0 / math.sqrt(L * GROUP_DIM)
    pos = N_META + jnp.arange(S, dtype=jnp.int32)
    ang = ((pos[:, None] * pos[None, :]) % L).astype(F32) * (2.0 * math.pi / L)
    cmat = (jnp.cos(ang) * scale).astype(BF16)
    smat = (jnp.sin(ang) * (-scale)).astype(BF16)
    angm = ((pos[:, None] * jnp.arange(N_META, dtype=jnp.int32)[None, :]) % L).astype(F32) \
        * (2.0 * math.pi / L)
    cm = jnp.concatenate([jnp.cos(angm) * scale, jnp.sin(angm) * (-scale)], axis=1).astype(BF16)
    return cmat, smat, cm


def _seq_dft(yc, ys, ymeta, B, S, tmo, tn):
    T = B * S
    nk = S // tmo
    cmat, smat, cm = _dft_matrices(S)
    return pl.pallas_call(
        functools.partial(_seq_dft_kernel, tn=tn),
        grid=(B, nk, S // tn),
        in_specs=[
            pl.BlockSpec((tmo, tn), lambda b, kt, nt: (kt, nt)),
            pl.BlockSpec((tmo, tn), lambda b, kt, nt: (kt, nt)),
            pl.BlockSpec((tmo, 2 * N_META), lambda b, kt, nt: (kt, 0)),
            pl.BlockSpec((2 * N_META, FOURIER_WIDTH), lambda b, kt, nt: (0, 0)),
            pl.BlockSpec((S, FOURIER_WIDTH), lambda b, kt, nt: (b, 0)),
            pl.BlockSpec((S, FOURIER_WIDTH), lambda b, kt, nt: (b, 0)),
        ],
        out_specs=pl.BlockSpec((tmo, FOURIER_WIDTH), lambda b, kt, nt: (b * nk + kt, 0)),
        out_shape=jax.ShapeDtypeStruct((T, FOURIER_WIDTH), BF16),
        scratch_shapes=[pltpu.VMEM((tmo, FOURIER_WIDTH), F32)],
        compiler_params=_cparams(("arbitrary", "arbitrary", "arbitrary")),
        name="seq_dft",
    )(cmat, smat, cm, ymeta, yc, ys)


def _merge_kernel(attn_ref, four_ref, sga_ref, sgf_ref, x_ref, wab_ref, wfb_ref, wout_ref,
                  gffn_ref, wrh_ref, wrl_ref, br_ref, h_ref, n2_ref, gate_ref, idx_ref):
    a = jnp.dot(attn_ref[...], wab_ref[...], preferred_element_type=F32)
    fb = jnp.dot(four_ref[...], wfb_ref[...], preferred_element_type=F32)
    mixed = sga_ref[...].astype(F32) * a + sgf_ref[...].astype(F32) * fb
    h = x_ref[...] + jnp.dot(mixed.astype(BF16), wout_ref[...], preferred_element_type=F32)
    h_ref[...] = h
    ms = jnp.mean(h * h, axis=-1, keepdims=True)
    n2 = (h * lax.rsqrt(ms + NORM_EPS)) * gffn_ref[...]
    n2_ref[...] = n2
    nh = n2.astype(BF16)
    nl = (n2 - nh.astype(F32)).astype(BF16)
    lg = (jnp.dot(nh, wrh_ref[...], preferred_element_type=F32)
          + jnp.dot(nl, wrh_ref[...], preferred_element_type=F32)
          + jnp.dot(nh, wrl_ref[...], preferred_element_type=F32)) + br_ref[...]
    col = lax.broadcasted_iota(jnp.int32, lg.shape, 1)
    colf = col.astype(F32)
    vals = []
    idx_out = jnp.zeros(lg.shape, jnp.int32)
    for kk in range(TOP_K):
        mx = jnp.max(lg, axis=-1, keepdims=True)
        am = jnp.min(jnp.where(lg == mx, colf, float(LANES)), axis=-1, keepdims=True)
        am = am.astype(jnp.int32)
        vals.append(mx)
        idx_out = jnp.where(col == kk, am, idx_out)
        lg = jnp.where(col == am, -jnp.inf, lg)
    es = [jnp.exp(v - vals[0]) for v in vals]
    den = es[0] + es[1] + es[2] + es[3]
    gate_out = jnp.zeros(lg.shape, F32)
    for kk in range(TOP_K):
        gate_out = jnp.where(col == kk, es[kk] / den, gate_out)
    gate_ref[...] = gate_out
    idx_ref[...] = idx_out


def _merge(attn, four, sga, sgf, x2d, wab, wfb, wout, gffn, wrh, wrl, br, tm):
    T = x2d.shape[0]
    row = lambda w: pl.BlockSpec((tm, w), lambda i: (i, 0))
    const = lambda a: pl.BlockSpec(a.shape, lambda i: (0,) * a.ndim)
    return pl.pallas_call(
        _merge_kernel,
        grid=(T // tm,),
        in_specs=[row(ATTN_WIDTH), row(FOURIER_WIDTH), row(D_MODEL), row(D_MODEL), row(D_MODEL),
                  const(wab), const(wfb), const(wout), const(gffn), const(wrh), const(wrl),
                  const(br)],
        out_specs=[row(D_MODEL), row(D_MODEL), row(LANES), row(LANES)],
        out_shape=[jax.ShapeDtypeStruct((T, D_MODEL), F32),
                   jax.ShapeDtypeStruct((T, D_MODEL), F32),
                   jax.ShapeDtypeStruct((T, LANES), F32),
                   jax.ShapeDtypeStruct((T, LANES), jnp.int32)],
        compiler_params=_cparams(("arbitrary",)),
        name="merge",
    )(attn, four, sga, sgf, x2d, wab, wfb, wout, gffn, wrh, wrl, br)


def _row_copy(src_ref, src_row, dst_ref, dst_row, sem):
    return pltpu.make_async_copy(src_ref.at[pl.ds(src_row, 1), :],
                                 dst_ref.at[pl.ds(dst_row, 1), :], sem)


DUMP_RING = 4


def _moe_kernel(be_ref, idx_hbm, x_hbm, wgu_ref, bgu_ref, wd_ref, bd_ref, y_hbm,
                idx_s, xbuf0, xbuf1, ybuf0, ybuf1, sem, *, bm, nb):
    del be_ref
    i = pl.program_id(0)

    def fetch_idx(row, slot):
        return pltpu.make_async_copy(idx_hbm.at[row], idx_s.at[slot], sem.at[4])

    def gather_all(xb, s):
        return pltpu.make_async_copy(x_hbm.at[pl.ds(0, bm), :], xb, sem.at[s])

    def scatter_all(yb, s):
        return pltpu.make_async_copy(yb, y_hbm.at[pl.ds(0, bm), :], sem.at[s])

    @pl.when(i == 0)
    def _():
        c0 = fetch_idx(0, 0)
        c0.start()
        c0.wait()

        def g0(r, c):
            _row_copy(x_hbm, idx_s[0, 0, r], xbuf0, r, sem.at[0]).start()
            return c

        lax.fori_loop(0, bm, g0, 0, unroll=8)
        fetch_idx(1, 1).start()
        ybuf1[...] = jnp.zeros_like(ybuf1)

    def step(p):
        xb_cur, xb_nxt = (xbuf0, xbuf1) if p == 0 else (xbuf1, xbuf0)
        yb_cur, yb_prv = (ybuf0, ybuf1) if p == 0 else (ybuf1, ybuf0)
        slot = 1 - p
        fetch_idx(0, slot).wait()
        fetch_idx(i + 2, p).start()
        gather_all(xb_cur, p).wait()

        @pl.when(i >= 1)
        def _():
            scatter_all(yb_cur, 2 + p).wait()

        x = xb_cur[...].astype(BF16)
        acts = []
        for t in range(D_FF // LANES):
            lo = 2 * LANES * t
            h = jnp.dot(x, wgu_ref[:, lo:lo + 2 * LANES], preferred_element_type=F32) \
                + bgu_ref[:, lo:lo + 2 * LANES]
            g = jnp.minimum(h[:, :LANES], SWIGLU_LIMIT)
            u = jnp.clip(h[:, LANES:], -SWIGLU_LIMIT, SWIGLU_LIMIT)
            acts.append(((u + 1.0) * (g * jax.nn.sigmoid(g * SWIGLU_ALPHA))).astype(BF16))
        act = jnp.concatenate(acts, axis=1)
        yb_cur[...] = jnp.dot(act, wd_ref[...], preferred_element_type=F32) + bd_ref[...]

        for r in range(bm):
            _row_copy(x_hbm, idx_s[slot, 0, r], xb_nxt, r, sem.at[1 - p]).start()
        for r in range(bm):
            _row_copy(yb_prv, r, y_hbm, idx_s[slot, 1, r], sem.at[3 - p]).start()

        @pl.when(i == nb - 1)
        def _():
            gather_all(xb_nxt, 1 - p).wait()
            fetch_idx(0, p).wait()

            def s1(r, c):
                _row_copy(yb_cur, r, y_hbm, idx_s[p, 1, r], sem.at[2 + p]).start()
                return c

            lax.fori_loop(0, bm, s1, 0, unroll=8)
            scatter_all(yb_prv, 3 - p).wait()
            scatter_all(yb_cur, 2 + p).wait()

    @pl.when(i % 2 == 0)
    def _():
        step(0)

    @pl.when(i % 2 == 1)
    def _():
        step(1)


def _moe(block_e, idx_rows, n2, wgu, bgu, wd, bd, n_rows_out, bm):
    nb = block_e.shape[0]
    anyspec = pl.BlockSpec(memory_space=pl.ANY)
    xy = pltpu.VMEM((bm, D_MODEL), F32)
    return pl.pallas_call(
        functools.partial(_moe_kernel, bm=bm, nb=nb),
        grid_spec=pltpu.PrefetchScalarGridSpec(
            num_scalar_prefetch=1,
            grid=(nb,),
            in_specs=[anyspec, anyspec,
                      pl.BlockSpec((None, D_MODEL, 2 * D_FF), lambda i, be: (be[i], 0, 0)),
                      pl.BlockSpec((None, 1, 2 * D_FF), lambda i, be: (be[i], 0, 0)),
                      pl.BlockSpec((None, D_FF, D_MODEL), lambda i, be: (be[i], 0, 0)),
                      pl.BlockSpec((None, 1, D_MODEL), lambda i, be: (be[i], 0, 0))],
            out_specs=anyspec,
            scratch_shapes=[pltpu.SMEM((2, 2, bm), jnp.int32), xy, xy, xy, xy,
                            pltpu.SemaphoreType.DMA((5,))]),
        out_shape=jax.ShapeDtypeStruct((n_rows_out, D_MODEL), F32),
        compiler_params=_cparams(("arbitrary",)),
        name="moe",
    )(block_e, idx_rows, n2, wgu, bgu, wd, bd)


def _routing(top_idx, T, bm):
    A = T * TOP_K
    flat_e = top_idx.reshape(-1)
    order = jnp.argsort(flat_e, stable=True).astype(jnp.int32)
    counts = jnp.sum((flat_e[:, None] == jnp.arange(N_EXPERTS, dtype=jnp.int32)[None, :])
                     .astype(jnp.int32), axis=0)
    nblk_e = (counts + bm - 1) // bm
    blk_end = jnp.cumsum(nblk_e)
    blk_start = blk_end - nblk_e
    grp_start = jnp.cumsum(counts) - counts
    nb = -(-A // bm) + N_EXPERTS
    blk = jnp.arange(nb, dtype=jnp.int32)
    block_e = jnp.minimum(jnp.sum((blk[:, None] >= blk_end[None, :]).astype(jnp.int32), axis=1),
                          N_EXPERTS - 1).astype(jnp.int32)
    lane = jnp.arange(bm, dtype=jnp.int32)[None, :]
    offs = (blk - blk_start[block_e])[:, None] * bm + lane
    valid = offs < counts[block_e][:, None]
    a = order[jnp.clip(grp_start[block_e][:, None] + offs, 0, A - 1)]
    row_src = jnp.where(valid, a // TOP_K, 0).astype(jnp.int32)
    dump = A + (blk % DUMP_RING)[:, None] * bm + lane
    row_dst = jnp.where(valid, (a % TOP_K) * T + a // TOP_K, dump).astype(jnp.int32)
    pre = jnp.broadcast_to(A + (DUMP_RING - 1) * bm + lane, (2, bm)).astype(jnp.int32)
    src_rows = jnp.concatenate([row_src, jnp.zeros((2, bm), jnp.int32)], axis=0)
    dst_rows = jnp.concatenate([pre, row_dst], axis=0)
    return block_e, jnp.stack([src_rows, dst_rows], axis=1)


def _final_kernel(h_ref, y0_ref, y1_ref, y2_ref, y3_ref, gate_ref, g_ref, o_ref):
    acc = h_ref[...]
    gate = gate_ref[...]
    for kk, y_ref in enumerate((y0_ref, y1_ref, y2_ref, y3_ref)):
        acc = acc + gate[:, kk:kk + 1] * y_ref[...]
    ms = jnp.mean(acc * acc, axis=-1, keepdims=True)
    o_ref[...] = (acc * lax.rsqrt(ms + NORM_EPS)) * g_ref[...]


def _final(h, y, gate, g_final, tm):
    T = h.shape[0]
    nt = T // tm
    yspec = lambda kk: pl.BlockSpec((tm, D_MODEL), lambda i: (kk * nt + i, 0))
    return pl.pallas_call(
        _final_kernel,
        grid=(nt,),
        in_specs=[pl.BlockSpec((tm, D_MODEL), lambda i: (i, 0)),
                  yspec(0), yspec(1), yspec(2), yspec(3),
                  pl.BlockSpec((tm, LANES), lambda i: (i, 0)),
                  pl.BlockSpec((1, D_MODEL), lambda i: (0, 0))],
        out_specs=pl.BlockSpec((tm, D_MODEL), lambda i: (i, 0)),
        out_shape=jax.ShapeDtypeStruct((T, D_MODEL), F32),
        compiler_params=_cparams(("arbitrary",)),
        name="final",
    )(h, y, y, y, y, gate, g_final)


def _prep_gate_up_kernel(w_ref, p_ref, o_ref):
    for t in range(D_FF // LANES):
        lo = 2 * LANES * t
        o_ref[:, lo:lo + 2 * LANES] = jnp.dot(w_ref[:, lo:lo + 2 * LANES].astype(BF16), p_ref[...],
                                              preferred_element_type=F32).astype(BF16)


def _prep_gate_up(w_gate_up):
    E = w_gate_up.shape[0]
    j = jnp.arange(2 * LANES, dtype=jnp.int32)
    src_col = jnp.where(j < LANES, 2 * j, 2 * (j - LANES) + 1)
    perm = (jnp.arange(2 * LANES, dtype=jnp.int32)[:, None] == src_col[None, :]).astype(BF16)
    return pl.pallas_call(
        _prep_gate_up_kernel,
        grid=(E,),
        in_specs=[pl.BlockSpec((None, D_MODEL, 2 * D_FF), lambda e: (e, 0, 0)),
                  pl.BlockSpec((2 * LANES, 2 * LANES), lambda e: (0, 0))],
        out_specs=pl.BlockSpec((None, D_MODEL, 2 * D_FF), lambda e: (e, 0, 0)),
        out_shape=jax.ShapeDtypeStruct((E, D_MODEL, 2 * D_FF), BF16),
        compiler_params=_cparams(("arbitrary",)),
        name="prep_gate_up",
    )(w_gate_up, perm)


def _encode_tokens(x, meta_tokens, g_mix, w_in, lambda_q1, lambda_k1, lambda_q2, lambda_k2,
                   g_subln, rel_bias, w_attn_branch, w_fourier_branch, w_out, g_ffn, w_router,
                   b_router, w_gate_up, b_gate_up, w_down, b_down, g_final):
    B, S, D = x.shape
    T = B * S
    tm = min(512, T)
    tq, rq, tk = (min(t, S) for t in ATTN_TILES)
    tdft = min(1024, S)
    bm = 256

    x2d = x.reshape(T, D)
    gm = g_mix[0].reshape(1, D)
    w_in_b = w_in[0].astype(BF16)
    cc = jnp.arange(GROUP_DIM, dtype=jnp.int32)
    angc = ((cc[:, None] * cc[None, :]) % GROUP_DIM).astype(F32) * (2.0 * math.pi / GROUP_DIM)
    dft_cs = jnp.concatenate([jnp.cos(angc), jnp.sin(angc)], axis=1).astype(BF16)

    q, k, v, yc, ys, sga, sgf = _in_proj(x2d, gm, w_in_b, dft_cs, tm)
    _, k_m, v_m, yc_m, ys_m, _, _ = _in_proj(meta_tokens.astype(F32), gm, w_in_b, dft_cs, N_META)

    lam = (jnp.exp(jnp.sum(lambda_q1[0].astype(F32) * lambda_k1[0].astype(F32)))
           - jnp.exp(jnp.sum(lambda_q2[0].astype(F32) * lambda_k2[0].astype(F32))) + LAM_INIT)
    pad = ((0, LANES - N_META), (0, 0))
    attn = _attention(lam.reshape(1), q, k, v, jnp.pad(k_m, pad), jnp.pad(v_m, pad), rel_bias,
                      g_subln[0].reshape(1, V_DIM), B, S, tq, rq, tk)

    ymeta = jnp.concatenate([yc_m, ys_m], axis=0)
    four = _seq_dft(yc, ys, ymeta, B, S, tdft, tdft)

    wr = jnp.pad(w_router[0], ((0, 0), (0, LANES - N_EXPERTS)))
    wrh = wr.astype(BF16)
    wrl = (wr - wrh.astype(F32)).astype(BF16)
    br = jnp.pad(b_router[0], (0, LANES - N_EXPERTS), constant_values=NEG_BIG).reshape(1, LANES)
    h, n2, gate, idx = _merge(attn, four, sga, sgf, x2d, w_attn_branch[0].astype(BF16),
                              w_fourier_branch[0].astype(BF16), w_out[0].astype(BF16),
                              g_ffn[0].reshape(1, D), wrh, wrl, br, tm)

    block_e, idx_rows = _routing(idx[:, :TOP_K], T, bm)
    wgu = _prep_gate_up(w_gate_up[0])
    ng = D_FF // LANES
    bgu = b_gate_up[0].reshape(N_EXPERTS, ng, LANES, 2).transpose(0, 1, 3, 2) \
        .reshape(N_EXPERTS, 1, 2 * D_FF)
    y = _moe(block_e, idx_rows, n2, wgu, bgu, w_down[0].astype(BF16), b_down[0][:, None, :],
             TOP_K * T + DUMP_RING * bm, bm)

    out = _final(h, y, gate, g_final.reshape(1, D), min(256, T))
    return out.reshape(B, S, D)


def kernel(x_prompt, x_sample, meta_tokens, g_mix, w_in, lambda_q1, lambda_k1, lambda_q2, lambda_k2, g_subln, rel_bias, w_attn_branch, w_fourier_branch, w_out, g_ffn, w_router, b_router, w_gate_up, b_gate_up, w_down, b_down, g_final):
    nb = x_prompt.shape[0]
    x = jnp.concatenate([x_prompt, x_sample], axis=0)
    out = _encode_tokens(x, meta_tokens, g_mix, w_in, lambda_q1, lambda_k1, lambda_q2, lambda_k2,
                         g_subln, rel_bias, w_attn_branch, w_fourier_branch, w_out, g_ffn,
                         w_router, b_router, w_gate_up, b_gate_up, w_down, b_down, g_final)
    return out[:nb], out[nb:]
```

```python
import functools
import math

import jax
import jax.numpy as jnp
from jax import lax
from jax.experimental import pallas as pl
from jax.experimental.pallas import tpu as pltpu

F32 = jnp.float32
BF16 = jnp.bfloat16

D_MODEL = 1024
N_META = 16
N_HEADS = 8
HEAD_DIM = 64
V_DIM = 128
QK_WIDTH = N_HEADS * 2 * HEAD_DIM
ATTN_WIDTH = N_HEADS * V_DIM
N_GROUPS = 4
GROUP_DIM = 128
FOURIER_WIDTH = N_GROUPS * GROUP_DIM
IN_COLS = 2 * QK_WIDTH + ATTN_WIDTH + FOURIER_WIDTH + 2 * D_MODEL
NUM_BUCKETS = 32
MAX_DISTANCE = 128
N_EXPERTS = 32
TOP_K = 4
D_FF = D_MODEL
SWIGLU_LIMIT = 7.0
SWIGLU_ALPHA = 1.702
NORM_EPS = 1e-6
SUBLN_EPS = 1e-5
LAM_INIT = 0.8 - 0.6 * math.exp(-0.3 * 0)
FAR_REL = 91
NEG_BIG = -1e30
LOG2E = math.log2(math.e)
LANES = 128
ATTN_TILES = (4096, 256, 512)
VMEM_LIMIT = 56 * 1024 * 1024


def _cparams(sem):
    return pltpu.CompilerParams(dimension_semantics=sem, vmem_limit_bytes=VMEM_LIMIT)


def _in_proj_kernel(x_ref, g_ref, w_ref, dft_ref, q_ref, k_ref, v_ref, yc_ref, ys_ref,
                    sga_ref, sgf_ref):
    x = x_ref[...]
    ms = jnp.mean(x * x, axis=-1, keepdims=True)
    nb = ((x * lax.rsqrt(ms + NORM_EPS)) * g_ref[...]).astype(BF16)

    def proj(lo, hi):
        return jnp.dot(nb, w_ref[:, lo:hi], preferred_element_type=F32)

    c0 = QK_WIDTH
    c1 = 2 * QK_WIDTH
    c2 = c1 + ATTN_WIDTH
    c3 = c2 + FOURIER_WIDTH
    c4 = c3 + D_MODEL
    q_ref[...] = (proj(0, c0) * (HEAD_DIM ** -0.5 * LOG2E)).astype(BF16)
    k_ref[...] = proj(c0, c1).astype(BF16)
    v_ref[...] = proj(c1, c2).astype(BF16)
    f = proj(c2, c3).astype(BF16)
    for g in range(N_GROUPS):
        lo, hi = g * GROUP_DIM, (g + 1) * GROUP_DIM
        y = jnp.dot(f[:, lo:hi], dft_ref[...], preferred_element_type=F32)
        yc_ref[:, lo:hi] = y[:, :GROUP_DIM].astype(BF16)
        ys_ref[:, lo:hi] = y[:, GROUP_DIM:].astype(BF16)
    sga_ref[...] = jax.nn.sigmoid(proj(c3, c4)).astype(BF16)
    sgf_ref[...] = jax.nn.sigmoid(proj(c4, IN_COLS)).astype(BF16)


def _in_proj(x2d, g_mix, w_in, dft_cs, tm):
    T = x2d.shape[0]
    row = lambda w: pl.BlockSpec((tm, w), lambda i: (i, 0))
    const = lambda a: pl.BlockSpec(a.shape, lambda i: (0,) * a.ndim)
    widths = (QK_WIDTH, QK_WIDTH, ATTN_WIDTH, FOURIER_WIDTH, FOURIER_WIDTH, D_MODEL, D_MODEL)
    return pl.pallas_call(
        _in_proj_kernel,
        grid=(T // tm,),
        in_specs=[row(D_MODEL), const(g_mix), const(w_in), const(dft_cs)],
        out_specs=[row(w) for w in widths],
        out_shape=[jax.ShapeDtypeStruct((T, w), BF16) for w in widths],
        compiler_params=_cparams(("arbitrary",)),
        name="in_proj",
    )(x2d, g_mix, w_in, dft_cs)


def _attn_kernel(lam_ref, q_ref, k_ref, v_ref, km_ref, vm_ref, bt_ref, bmeta_ref, g_ref, o_ref,
                 m_sc, acc_sc, s0_sc, *, tq, rq, tk, nkc, d_lo, d_hi):
    qi = pl.program_id(2)
    nsub = tq // rq
    nt = (((1,), (1,)), ((), ()))
    col = lax.broadcasted_iota(jnp.int32, (rq, V_DIM), 1)
    chains = []
    for j in range(nsub):
        q = q_ref[j * rq:(j + 1) * rq, :]
        zero = jnp.zeros_like(q)
        chains.append((j * rq, jnp.where(col < HEAD_DIM, q, zero), j))
        chains.append((tq + j * rq, jnp.where(col >= HEAD_DIM, q, zero), j))
    nch = len(chains)
    ones_col = jnp.where(lax.broadcasted_iota(jnp.int32, (tk, LANES), 1) == 0, 1.0, 0.0).astype(BF16)
    ones_m = jnp.where(lax.broadcasted_iota(jnp.int32, (LANES, LANES), 1) == 0, 1.0, 0.0).astype(BF16)

    def scores(qc, c):
        kc = k_ref[pl.ds(pl.multiple_of(c * tk, tk), tk), :]
        return lax.dot_general(qc, kc, nt, preferred_element_type=F32)

    s0_sc[...] = scores(chains[0][1], 0)

    km = km_ref[...]
    vm = jnp.concatenate([vm_ref[...], ones_m], axis=1)
    for row0, qc, j in chains:
        rows = pl.ds(row0, rq)
        bmeta = bmeta_ref[jnp.minimum(qi * nsub + j, 1)]
        s = lax.dot_general(qc, km, nt, preferred_element_type=F32) + bmeta
        m0 = jnp.max(s, axis=-1, keepdims=True)
        p = jnp.exp2((s - m0).astype(BF16))
        m_sc[rows, :] = jnp.broadcast_to(m0, (rq, LANES))
        acc_sc[rows, :] = jnp.dot(p, vm, preferred_element_type=F32)

    def body(c, carry):
        off = pl.multiple_of(c * tk, tk)
        vc = jnp.concatenate([v_ref[pl.ds(off, tk), :], ones_col], axis=1)
        s = s0_sc[...]
        for n, (row0, qc, j) in enumerate(chains):
            if n + 1 < nch:
                s_ahead = scores(chains[n + 1][1], c)
            else:
                s_ahead = scores(chains[0][1], jnp.minimum(c + 1, nkc - 1))
            rows = pl.ds(row0, rq)
            bi = jnp.clip(c * (tk // rq) - (qi * nsub + j), d_lo, d_hi) - d_lo
            s = s + bt_ref[bi]
            m_prev = m_sc[rows, :]
            m_next = jnp.maximum(m_prev, jnp.max(s, axis=-1, keepdims=True))
            alpha = jnp.exp2(m_prev - m_next)
            p = jnp.exp2((s - jnp.tile(m_next, (1, tk // LANES))).astype(BF16))
            acc_sc[rows, :] = jnp.tile(alpha, (1, 2)) * acc_sc[rows, :] \
                + jnp.dot(p, vc, preferred_element_type=F32)
            m_sc[rows, :] = m_next
            s = s_ahead
        s0_sc[...] = s
        return carry

    lax.fori_loop(0, nkc, body, 0)
    acc = acc_sc[...]
    o = acc[:, :V_DIM] / jnp.sum(acc[:, V_DIM:], axis=-1, keepdims=True)
    o = o[:tq] - lam_ref[0] * o[tq:]
    ms = jnp.mean(o * o, axis=-1, keepdims=True)
    o = (o * lax.rsqrt(ms + SUBLN_EPS)) * g_ref[...] * (1.0 - LAM_INIT)
    o_ref[...] = o.astype(BF16)


def _bias_range(tq, tk):
    d_lo = -((tk - 1 + FAR_REL + tq - 1) // tq)
    d_hi = (FAR_REL + tq - 1 + tq - 1) // tq
    return d_lo, d_hi


def _relative_bucket(rel):
    half = NUM_BUCKETS // 2
    max_exact = half // 2
    n = jnp.abs(rel)
    large = max_exact + (jnp.log(jnp.maximum(n, 1).astype(F32) / max_exact)
                         / math.log(MAX_DISTANCE / max_exact) * (half - max_exact)).astype(jnp.int32)
    large = jnp.minimum(large, half - 1)
    return jnp.where(rel > 0, half, 0) + jnp.where(n < max_exact, n, large)


def _bias_tables(rel_bias, rq, tk):
    d_lo, d_hi = _bias_range(rq, tk)
    rb = rel_bias.astype(F32) * LOG2E

    def lookup(rel):
        onehot = (_relative_bucket(rel)[..., None]
                  == jnp.arange(NUM_BUCKETS, dtype=jnp.int32)).astype(F32)
        return jnp.einsum('xyzb,bh->hxyz', onehot, rb, precision=lax.Precision.HIGHEST)

    r = jnp.arange(rq, dtype=jnp.int32)[None, :, None]
    j = jnp.arange(tk, dtype=jnp.int32)[None, None, :]
    d = jnp.arange(d_lo, d_hi + 1, dtype=jnp.int32)[:, None, None]
    tab = lookup(j - r + d * rq)
    jm = jnp.arange(LANES, dtype=jnp.int32)[None, None, :]
    sb = jnp.arange(2, dtype=jnp.int32)[:, None, None]
    tabm = jnp.where(jm < N_META, lookup(jm - (N_META + sb * rq + r)), NEG_BIG)
    return tab, tabm, d_lo, d_hi


def _attention(lam, q, k, v, km, vm, rel_bias, g_subln, B, S, tq, rq, tk):
    T = B * S
    nq = S // tq
    tab, tabm, d_lo, d_hi = _bias_tables(rel_bias, rq, tk)
    nd = d_hi - d_lo + 1
    kern = functools.partial(_attn_kernel, tq=tq, rq=rq, tk=tk, nkc=S // tk, d_lo=d_lo, d_hi=d_hi)
    return pl.pallas_call(
        kern,
        grid=(B, N_HEADS, nq),
        in_specs=[
            pl.BlockSpec(memory_space=pltpu.SMEM),
            pl.BlockSpec((tq, V_DIM), lambda b, h, i: (b * nq + i, h)),
            pl.BlockSpec((S, V_DIM), lambda b, h, i: (b, h)),
            pl.BlockSpec((S, V_DIM), lambda b, h, i: (b, h)),
            pl.BlockSpec((LANES, V_DIM), lambda b, h, i: (0, h)),
            pl.BlockSpec((LANES, V_DIM), lambda b, h, i: (0, h)),
            pl.BlockSpec((None, nd, rq, tk), lambda b, h, i: (h, 0, 0, 0)),
            pl.BlockSpec((None, 2, rq, LANES), lambda b, h, i: (h, 0, 0, 0)),
            pl.BlockSpec((1, V_DIM), lambda b, h, i: (0, 0)),
        ],
        out_specs=pl.BlockSpec((tq, V_DIM), lambda b, h, i: (b * nq + i, h)),
        out_shape=jax.ShapeDtypeStruct((T, ATTN_WIDTH), BF16),
        scratch_shapes=[pltpu.VMEM((2 * tq, LANES), F32),
                        pltpu.VMEM((2 * tq, 2 * V_DIM), F32),
                        pltpu.VMEM((rq, tk), F32)],
        compiler_params=_cparams(("arbitrary", "arbitrary", "arbitrary")),
        name="attn",
    )(lam, q, k, v, km, vm, tab, tabm, g_subln)


def _seq_dft_kernel(c_ref, s_ref, cm_ref, ym_ref, yc_ref, ys_ref, o_ref, acc_sc, *, tn):
    nt = pl.program_id(2)

    @pl.when(nt == 0)
    def _():
        acc_sc[...] = jnp.dot(cm_ref[...], ym_ref[...], preferred_element_type=F32)

    off = pl.multiple_of(nt * tn, tn)
    acc_sc[...] += (jnp.dot(c_ref[...], yc_ref[pl.ds(off, tn), :], preferred_element_type=F32)
                    + jnp.dot(s_ref[...], ys_ref[pl.ds(off, tn), :], preferred_element_type=F32))

    @pl.when(nt == pl.num_programs(2) - 1)
    def _():
        o_ref[...] = acc_sc[...].astype(BF16)


def _dft_matrices(S):
    L = S + N_META
    scale = 1.0 / math.sqrt(L * GROUP_DIM)
    pos = N_META + jnp.arange(S, dtype=jnp.int32)
    ang = ((pos[:, None] * pos[None, :]) % L).astype(F32) * (2.0 * math.pi / L)
    cmat = (jnp.cos(ang) * scale).astype(BF16)
    smat = (jnp.sin(ang) * (-scale)).astype(BF16)
    angm = ((pos[:, None] * jnp.arange(N_META, dtype=jnp.int32)[None, :]) % L).astype(F32) \
        * (2.0 * math.pi / L)
    cm = jnp.concatenate([jnp.cos(angm) * scale, jnp.sin(angm) * (-scale)], axis=1).astype(BF16)
    return cmat, smat, cm


def _seq_dft(yc, ys, ymeta, B, S, tmo, tn):
    T = B * S
    nk = S // tmo
    cmat, smat, cm = _dft_matrices(S)
    return pl.pallas_call(
        functools.partial(_seq_dft_kernel, tn=tn),
        grid=(B, nk, S // tn),
        in_specs=[
            pl.BlockSpec((tmo, tn), lambda b, kt, nt: (kt, nt)),
            pl.BlockSpec((tmo, tn), lambda b, kt, nt: (kt, nt)),
            pl.BlockSpec((tmo, 2 * N_META), lambda b, kt, nt: (kt, 0)),
            pl.BlockSpec((2 * N_META, FOURIER_WIDTH), lambda b, kt, nt: (0, 0)),
            pl.BlockSpec((S, FOURIER_WIDTH), lambda b, kt, nt: (b, 0)),
            pl.BlockSpec((S, FOURIER_WIDTH), lambda b, kt, nt: (b, 0)),
        ],
        out_specs=pl.BlockSpec((tmo, FOURIER_WIDTH), lambda b, kt, nt: (b * nk + kt, 0)),
        out_shape=jax.ShapeDtypeStruct((T, FOURIER_WIDTH), BF16),
        scratch_shapes=[pltpu.VMEM((tmo, FOURIER_WIDTH), F32)],
        compiler_params=_cparams(("arbitrary", "arbitrary", "arbitrary")),
        name="seq_dft",
    )(cmat, smat, cm, ymeta, yc, ys)


def _merge_kernel(attn_ref, four_ref, sga_ref, sgf_ref, x_ref, wab_ref, wfb_ref, wout_ref,
                  gffn_ref, wrh_ref, wrl_ref, br_ref, h_ref, n2_ref, gate_ref, idx_ref):
    a = jnp.dot(attn_ref[...], wab_ref[...], preferred_element_type=F32)
    fb = jnp.dot(four_ref[...], wfb_ref[...], preferred_element_type=F32)
    mixed = sga_ref[...].astype(F32) * a + sgf_ref[...].astype(F32) * fb
    h = x_ref[...] + jnp.dot(mixed.astype(BF16), wout_ref[...], preferred_element_type=F32)
    h_ref[...] = h
    ms = jnp.mean(h * h, axis=-1, keepdims=True)
    n2 = (h * lax.rsqrt(ms + NORM_EPS)) * gffn_ref[...]
    n2_ref[...] = n2
    nh = n2.astype(BF16)
    nl = (n2 - nh.astype(F32)).astype(BF16)
    lg = (jnp.dot(nh, wrh_ref[...], preferred_element_type=F32)
          + jnp.dot(nl, wrh_ref[...], preferred_element_type=F32)
          + jnp.dot(nh, wrl_ref[...], preferred_element_type=F32)) + br_ref[...]
    col = lax.broadcasted_iota(jnp.int32, lg.shape, 1)
    colf = col.astype(F32)
    vals = []
    idx_out = jnp.zeros(lg.shape, jnp.int32)
    for kk in range(TOP_K):
        mx = jnp.max(lg, axis=-1, keepdims=True)
        am = jnp.min(jnp.where(lg == mx, colf, float(LANES)), axis=-1, keepdims=True)
        am = am.astype(jnp.int32)
        vals.append(mx)
        idx_out = jnp.where(col == kk, am, idx_out)
        lg = jnp.where(col == am, -jnp.inf, lg)
    es = [jnp.exp(v - vals[0]) for v in vals]
    den = es[0] + es[1] + es[2] + es[3]
    gate_out = jnp.zeros(lg.shape, F32)
    for kk in range(TOP_K):
        gate_out = jnp.where(col == kk, es[kk] / den, gate_out)
    gate_ref[...] = gate_out
    idx_ref[...] = idx_out


def _merge(attn, four, sga, sgf, x2d, wab, wfb, wout, gffn, wrh, wrl, br, tm):
    T = x2d.shape[0]
    row = lambda w: pl.BlockSpec((tm, w), lambda i: (i, 0))
    const = lambda a: pl.BlockSpec(a.shape, lambda i: (0,) * a.ndim)
    return pl.pallas_call(
        _merge_kernel,
        grid=(T // tm,),
        in_specs=[row(ATTN_WIDTH), row(FOURIER_WIDTH), row(D_MODEL), row(D_MODEL), row(D_MODEL),
                  const(wab), const(wfb), const(wout), const(gffn), const(wrh), const(wrl),
                  const(br)],
        out_specs=[row(D_MODEL), row(D_MODEL), row(LANES), row(LANES)],
        out_shape=[jax.ShapeDtypeStruct((T, D_MODEL), F32),
                   jax.ShapeDtypeStruct((T, D_MODEL), F32),
                   jax.ShapeDtypeStruct((T, LANES), F32),
                   jax.ShapeDtypeStruct((T, LANES), jnp.int32)],
        compiler_params=_cparams(("arbitrary",)),
        name="merge",
    )(attn, four, sga, sgf, x2d, wab, wfb, wout, gffn, wrh, wrl, br)


def _row_copy(src_ref, src_row, dst_ref, dst_row, sem):
    return pltpu.make_async_copy(src_ref.at[pl.ds(src_row, 1), :],
                                 dst_ref.at[pl.ds(dst_row, 1), :], sem)


DUMP_RING = 4


NBUF = 3


def _moe_kernel(be_ref, idx_hbm, x_hbm, wgu_ref, bgu_ref, wd_ref, bd_ref, y_hbm,
                idx_s, xb0, xb1, xb2, yb0, yb1, yb2, sem, *, bm, nb):
    del be_ref
    i = pl.program_id(0)
    xbs = (xb0, xb1, xb2)
    ybs = (yb0, yb1, yb2)

    def fetch_idx(row, slot):
        return pltpu.make_async_copy(idx_hbm.at[row], idx_s.at[slot], sem.at[2 * NBUF])

    def gather_all(b):
        return pltpu.make_async_copy(x_hbm.at[pl.ds(0, bm), :], xbs[b], sem.at[b])

    def scatter_all(b):
        return pltpu.make_async_copy(ybs[b], y_hbm.at[pl.ds(0, bm), :], sem.at[NBUF + b])

    @pl.when(i == 0)
    def _():
        for b in range(2):
            c0 = fetch_idx(b, b)
            c0.start()
            c0.wait()

            def g0(r, c, b=b):
                _row_copy(x_hbm, idx_s[b, 0, r], xbs[b], r, sem.at[b]).start()
                return c

            lax.fori_loop(0, bm, g0, 0, unroll=8)
        fetch_idx(2, 2).start()
        yb2[...] = jnp.zeros_like(yb2)

    def step(p):
        nxt = (p + 2) % NBUF
        fetch_idx(0, nxt).wait()
        fetch_idx(i + 3, p).start()
        gather_all(p).wait()

        @pl.when(i >= 2)
        def _():
            scatter_all(p).wait()

        x = xbs[p][...].astype(BF16)
        acts = []
        for t in range(D_FF // LANES):
            lo = 2 * LANES * t
            h = jnp.dot(x, wgu_ref[:, lo:lo + 2 * LANES], preferred_element_type=F32) \
                + bgu_ref[:, lo:lo + 2 * LANES]
            g = jnp.minimum(h[:, :LANES], SWIGLU_LIMIT)
            u = jnp.clip(h[:, LANES:], -SWIGLU_LIMIT, SWIGLU_LIMIT)
            acts.append(((u + 1.0) * (g * jax.nn.sigmoid(g * SWIGLU_ALPHA))).astype(BF16))
        act = jnp.concatenate(acts, axis=1)
        ybs[p][...] = jnp.dot(act, wd_ref[...], preferred_element_type=F32) + bd_ref[...]

        for r in range(bm):
            _row_copy(x_hbm, idx_s[nxt, 0, r], xbs[nxt], r, sem.at[nxt]).start()
        for r in range(bm):
            _row_copy(ybs[nxt], r, y_hbm, idx_s[nxt, 1, r], sem.at[NBUF + nxt]).start()

        @pl.when(i == nb - 1)
        def _():
            gather_all((p + 1) % NBUF).wait()
            gather_all(nxt).wait()
            fetch_idx(0, p).wait()

            def s1(r, c):
                _row_copy(ybs[p], r, y_hbm, idx_s[p, 1, r], sem.at[NBUF + p]).start()
                return c

            lax.fori_loop(0, bm, s1, 0, unroll=8)
            for b in range(NBUF):
                scatter_all(b).wait()

    for p in range(NBUF):
        pl.when(i % NBUF == p)(functools.partial(step, p))


def _moe(block_e, idx_rows, n2, wgu, bgu, wd, bd, n_rows_out, bm):
    nb = block_e.shape[0]
    anyspec = pl.BlockSpec(memory_space=pl.ANY)
    xy = pltpu.VMEM((bm, D_MODEL), F32)
    return pl.pallas_call(
        functools.partial(_moe_kernel, bm=bm, nb=nb),
        grid_spec=pltpu.PrefetchScalarGridSpec(
            num_scalar_prefetch=1,
            grid=(nb,),
            in_specs=[anyspec, anyspec,
                      pl.BlockSpec((None, D_MODEL, 2 * D_FF), lambda i, be: (be[i], 0, 0)),
                      pl.BlockSpec((None, 1, 2 * D_FF), lambda i, be: (be[i], 0, 0)),
                      pl.BlockSpec((None, D_FF, D_MODEL), lambda i, be: (be[i], 0, 0)),
                      pl.BlockSpec((None, 1, D_MODEL), lambda i, be: (be[i], 0, 0))],
            out_specs=anyspec,
            scratch_shapes=[pltpu.SMEM((NBUF, 2, bm), jnp.int32)] + [xy] * (2 * NBUF)
            + [pltpu.SemaphoreType.DMA((2 * NBUF + 1,))]),
        out_shape=jax.ShapeDtypeStruct((n_rows_out, D_MODEL), F32),
        compiler_params=_cparams(("arbitrary",)),
        name="moe",
    )(block_e, idx_rows, n2, wgu, bgu, wd, bd)


def _routing(top_idx, T, bm):
    A = T * TOP_K
    flat_e = top_idx.reshape(-1)
    order = jnp.argsort(flat_e, stable=True).astype(jnp.int32)
    counts = jnp.sum((flat_e[:, None] == jnp.arange(N_EXPERTS, dtype=jnp.int32)[None, :])
                     .astype(jnp.int32), axis=0)
    nblk_e = (counts + bm - 1) // bm
    blk_end = jnp.cumsum(nblk_e)
    blk_start = blk_end - nblk_e
    grp_start = jnp.cumsum(counts) - counts
    nb = -(-A // bm) + N_EXPERTS
    blk = jnp.arange(nb, dtype=jnp.int32)
    block_e = jnp.minimum(jnp.sum((blk[:, None] >= blk_end[None, :]).astype(jnp.int32), axis=1),
                          N_EXPERTS - 1).astype(jnp.int32)
    lane = jnp.arange(bm, dtype=jnp.int32)[None, :]
    offs = (blk - blk_start[block_e])[:, None] * bm + lane
    valid = offs < counts[block_e][:, None]
    a = order[jnp.clip(grp_start[block_e][:, None] + offs, 0, A - 1)]
    row_src = jnp.where(valid, a // TOP_K, 0).astype(jnp.int32)
    dump = A + (blk % DUMP_RING)[:, None] * bm + lane
    row_dst = jnp.where(valid, (a % TOP_K) * T + a // TOP_K, dump).astype(jnp.int32)
    pre = jnp.broadcast_to(A + (DUMP_RING - 1) * bm + lane, (NBUF, bm)).astype(jnp.int32)
    src_rows = jnp.concatenate([row_src, jnp.zeros((NBUF, bm), jnp.int32)], axis=0)
    dst_rows = jnp.concatenate([pre, row_dst], axis=0)
    return block_e, jnp.stack([src_rows, dst_rows], axis=1)


def _final_kernel(h_ref, y0_ref, y1_ref, y2_ref, y3_ref, gate_ref, g_ref, o_ref):
    acc = h_ref[...]
    gate = gate_ref[...]
    for kk, y_ref in enumerate((y0_ref, y1_ref, y2_ref, y3_ref)):
        acc = acc + gate[:, kk:kk + 1] * y_ref[...]
    ms = jnp.mean(acc * acc, axis=-1, keepdims=True)
    o_ref[...] = (acc * lax.rsqrt(ms + NORM_EPS)) * g_ref[...]


def _final(h, y, gate, g_final, tm):
    T = h.shape[0]
    nt = T // tm
    yspec = lambda kk: pl.BlockSpec((tm, D_MODEL), lambda i: (kk * nt + i, 0))
    return pl.pallas_call(
        _final_kernel,
        grid=(nt,),
        in_specs=[pl.BlockSpec((tm, D_MODEL), lambda i: (i, 0)),
                  yspec(0), yspec(1), yspec(2), yspec(3),
                  pl.BlockSpec((tm, LANES), lambda i: (i, 0)),
                  pl.BlockSpec((1, D_MODEL), lambda i: (0, 0))],
        out_specs=pl.BlockSpec((tm, D_MODEL), lambda i: (i, 0)),
        out_shape=jax.ShapeDtypeStruct((T, D_MODEL), F32),
        compiler_params=_cparams(("arbitrary",)),
        name="final",
    )(h, y, y, y, y, gate, g_final)


def _prep_gate_up_kernel(w_ref, p_ref, o_ref):
    for t in range(D_FF // LANES):
        lo = 2 * LANES * t
        o_ref[:, lo:lo + 2 * LANES] = jnp.dot(w_ref[:, lo:lo + 2 * LANES].astype(BF16), p_ref[...],
                                              preferred_element_type=F32).astype(BF16)


def _prep_gate_up(w_gate_up):
    E = w_gate_up.shape[0]
    j = jnp.arange(2 * LANES, dtype=jnp.int32)
    src_col = jnp.where(j < LANES, 2 * j, 2 * (j - LANES) + 1)
    perm = (jnp.arange(2 * LANES, dtype=jnp.int32)[:, None] == src_col[None, :]).astype(BF16)
    return pl.pallas_call(
        _prep_gate_up_kernel,
        grid=(E,),
        in_specs=[pl.BlockSpec((None, D_MODEL, 2 * D_FF), lambda e: (e, 0, 0)),
                  pl.BlockSpec((2 * LANES, 2 * LANES), lambda e: (0, 0))],
        out_specs=pl.BlockSpec((None, D_MODEL, 2 * D_FF), lambda e: (e, 0, 0)),
        out_shape=jax.ShapeDtypeStruct((E, D_MODEL, 2 * D_FF), BF16),
        compiler_params=_cparams(("arbitrary",)),
        name="prep_gate_up",
    )(w_gate_up, perm)


def _encode_tokens(x, meta_tokens, g_mix, w_in, lambda_q1, lambda_k1, lambda_q2, lambda_k2,
                   g_subln, rel_bias, w_attn_branch, w_fourier_branch, w_out, g_ffn, w_router,
                   b_router, w_gate_up, b_gate_up, w_down, b_down, g_final):
    B, S, D = x.shape
    T = B * S
    tm = min(512, T)
    tq, rq, tk = (min(t, S) for t in ATTN_TILES)
    tdft = min(1024, S)
    bm = 256

    x2d = x.reshape(T, D)
    gm = g_mix[0].reshape(1, D)
    w_in_b = w_in[0].astype(BF16)
    cc = jnp.arange(GROUP_DIM, dtype=jnp.int32)
    angc = ((cc[:, None] * cc[None, :]) % GROUP_DIM).astype(F32) * (2.0 * math.pi / GROUP_DIM)
    dft_cs = jnp.concatenate([jnp.cos(angc), jnp.sin(angc)], axis=1).astype(BF16)

    q, k, v, yc, ys, sga, sgf = _in_proj(x2d, gm, w_in_b, dft_cs, tm)
    _, k_m, v_m, yc_m, ys_m, _, _ = _in_proj(meta_tokens.astype(F32), gm, w_in_b, dft_cs, N_META)

    lam = (jnp.exp(jnp.sum(lambda_q1[0].astype(F32) * lambda_k1[0].astype(F32)))
           - jnp.exp(jnp.sum(lambda_q2[0].astype(F32) * lambda_k2[0].astype(F32))) + LAM_INIT)
    pad = ((0, LANES - N_META), (0, 0))
    attn = _attention(lam.reshape(1), q, k, v, jnp.pad(k_m, pad), jnp.pad(v_m, pad), rel_bias,
                      g_subln[0].reshape(1, V_DIM), B, S, tq, rq, tk)

    ymeta = jnp.concatenate([yc_m, ys_m], axis=0)
    four = _seq_dft(yc, ys, ymeta, B, S, tdft, tdft)

    wr = jnp.pad(w_router[0], ((0, 0), (0, LANES - N_EXPERTS)))
    wrh = wr.astype(BF16)
    wrl = (wr - wrh.astype(F32)).astype(BF16)
    br = jnp.pad(b_router[0], (0, LANES - N_EXPERTS), constant_values=NEG_BIG).reshape(1, LANES)
    h, n2, gate, idx = _merge(attn, four, sga, sgf, x2d, w_attn_branch[0].astype(BF16),
                              w_fourier_branch[0].astype(BF16), w_out[0].astype(BF16),
                              g_ffn[0].reshape(1, D), wrh, wrl, br, tm)

    block_e, idx_rows = _routing(idx[:, :TOP_K], T, bm)
    wgu = _prep_gate_up(w_gate_up[0])
    ng = D_FF // LANES
    bgu = b_gate_up[0].reshape(N_EXPERTS, ng, LANES, 2).transpose(0, 1, 3, 2) \
        .reshape(N_EXPERTS, 1, 2 * D_FF)
    y = _moe(block_e, idx_rows, n2, wgu, bgu, w_down[0].astype(BF16), b_down[0][:, None, :],
             TOP_K * T + DUMP_RING * bm, bm)

    out = _final(h, y, gate, g_final.reshape(1, D), min(256, T))
    return out.reshape(B, S, D)


def kernel(x_prompt, x_sample, meta_tokens, g_mix, w_in, lambda_q1, lambda_k1, lambda_q2, lambda_k2, g_subln, rel_bias, w_attn_branch, w_fourier_branch, w_out, g_ffn, w_router, b_router, w_gate_up, b_gate_up, w_down, b_down, g_final):
    nb = x_prompt.shape[0]
    x = jnp.concatenate([x_prompt, x_sample], axis=0)
    out = _encode_tokens(x, meta_tokens, g_mix, w_in, lambda_q1, lambda_k1, lambda_q2, lambda_k2,
                         g_subln, rel_bias, w_attn_branch, w_fourier_branch, w_out, g_ffn,
                         w_router, b_router, w_gate_up, b_gate_up, w_down, b_down, g_final)
    return out[:nb], out[nb:]
```

```python
import functools
import math

import jax
import jax.numpy as jnp
from jax import lax
from jax.experimental import pallas as pl
from jax.experimental.pallas import tpu as pltpu

F32 = jnp.float32
BF16 = jnp.bfloat16

D_MODEL = 1024
N_META = 16
N_HEADS = 8
HEAD_DIM = 64
V_DIM = 128
QK_WIDTH = N_HEADS * 2 * HEAD_DIM
ATTN_WIDTH = N_HEADS * V_DIM
N_GROUPS = 4
GROUP_DIM = 128
FOURIER_WIDTH = N_GROUPS * GROUP_DIM
IN_COLS = 2 * QK_WIDTH + ATTN_WIDTH + FOURIER_WIDTH + 2 * D_MODEL
NUM_BUCKETS = 32
MAX_DISTANCE = 128
N_EXPERTS = 32
TOP_K = 4
D_FF = D_MODEL
SWIGLU_LIMIT = 7.0
SWIGLU_ALPHA = 1.702
NORM_EPS = 1e-6
SUBLN_EPS = 1e-5
LAM_INIT = 0.8 - 0.6 * math.exp(-0.3 * 0)
FAR_REL = 91
NEG_BIG = -1e30
LOG2E = math.log2(math.e)
LANES = 128
ATTN_TILES = (4096, 256, 512)
VMEM_LIMIT = 56 * 1024 * 1024


def _cparams(sem):
    return pltpu.CompilerParams(dimension_semantics=sem, vmem_limit_bytes=VMEM_LIMIT)


def _pick_rows(refs, n_first):
    if len(refs) == 1:
        return refs[0][...]
    return jnp.where(pl.program_id(0) < n_first, refs[0][...], refs[1][...])


def _row_specs(arrs, tm, width):
    n0 = arrs[0].shape[0] // tm
    specs = [pl.BlockSpec((tm, width), lambda i: (jnp.minimum(i, n0 - 1), 0))]
    if len(arrs) == 2:
        specs.append(pl.BlockSpec((tm, width), lambda i: (jnp.maximum(i - n0, 0), 0)))
    return specs, n0


def _in_proj_kernel(*refs, n_x, n_first):
    x_refs, (g_ref, w_ref, dft_ref, q_ref, k_ref, v_ref, yc_ref, ys_ref, sga_ref, sgf_ref) = \
        refs[:n_x], refs[n_x:]
    x = _pick_rows(x_refs, n_first)
    ms = jnp.mean(x * x, axis=-1, keepdims=True)
    nb = ((x * lax.rsqrt(ms + NORM_EPS)) * g_ref[...]).astype(BF16)

    def proj(lo, hi):
        return jnp.dot(nb, w_ref[:, lo:hi], preferred_element_type=F32)

    c0 = QK_WIDTH
    c1 = 2 * QK_WIDTH
    c2 = c1 + ATTN_WIDTH
    c3 = c2 + FOURIER_WIDTH
    c4 = c3 + D_MODEL
    q_ref[...] = (proj(0, c0) * (HEAD_DIM ** -0.5 * LOG2E)).astype(BF16)
    k_ref[...] = proj(c0, c1).astype(BF16)
    v_ref[...] = proj(c1, c2).astype(BF16)
    f = proj(c2, c3).astype(BF16)
    for g in range(N_GROUPS):
        lo, hi = g * GROUP_DIM, (g + 1) * GROUP_DIM
        y = jnp.dot(f[:, lo:hi], dft_ref[...], preferred_element_type=F32)
        yc_ref[:, lo:hi] = y[:, :GROUP_DIM].astype(BF16)
        ys_ref[:, lo:hi] = y[:, GROUP_DIM:].astype(BF16)
    sga_ref[...] = jax.nn.sigmoid(proj(c3, c4)).astype(BF16)
    sgf_ref[...] = jax.nn.sigmoid(proj(c4, IN_COLS)).astype(BF16)


def _in_proj(xs, g_mix, w_in, dft_cs, tm):
    T = sum(x.shape[0] for x in xs)
    row = lambda w: pl.BlockSpec((tm, w), lambda i: (i, 0))
    const = lambda a: pl.BlockSpec(a.shape, lambda i: (0,) * a.ndim)
    widths = (QK_WIDTH, QK_WIDTH, ATTN_WIDTH, FOURIER_WIDTH, FOURIER_WIDTH, D_MODEL, D_MODEL)
    x_specs, n_first = _row_specs(xs, tm, D_MODEL)
    return pl.pallas_call(
        functools.partial(_in_proj_kernel, n_x=len(xs), n_first=n_first),
        grid=(T // tm,),
        in_specs=x_specs + [const(g_mix), const(w_in), const(dft_cs)],
        out_specs=[row(w) for w in widths],
        out_shape=[jax.ShapeDtypeStruct((T, w), BF16) for w in widths],
        compiler_params=_cparams(("arbitrary",)),
        name="in_proj",
    )(*xs, g_mix, w_in, dft_cs)


def _attn_kernel(lam_ref, q_ref, k_ref, v_ref, km_ref, vm_ref, bt_ref, bmeta_ref, g_ref, o_ref,
                 m_sc, acc_sc, s0_sc, *, tq, rq, tk, nkc, d_lo, d_hi):
    qi = pl.program_id(2)
    nsub = tq // rq
    nt = (((1,), (1,)), ((), ()))
    q = q_ref[...]
    col = lax.broadcasted_iota(jnp.int32, (tq, V_DIM), 1)
    zero = jnp.zeros_like(q)
    qq = jnp.concatenate([jnp.where(col < HEAD_DIM, q, zero),
                          jnp.where(col >= HEAD_DIM, q, zero)], axis=0)
    chains = []
    for j in range(nsub):
        for row0 in (j * rq, tq + j * rq):
            chains.append((row0, qq[row0:row0 + rq], j))
    nch = len(chains)
    ones_col = jnp.ones((tk, LANES), BF16)
    ones_m = jnp.ones((LANES, LANES), BF16)

    def scores(qc, c):
        kc = k_ref[pl.ds(pl.multiple_of(c * tk, tk), tk), :]
        return lax.dot_general(qc, kc, nt, preferred_element_type=F32)

    s0_sc[...] = scores(chains[0][1], 0)

    vm = jnp.concatenate([vm_ref[...], ones_m], axis=1)
    bmeta = bmeta_ref[...]
    s = lax.dot_general(qq, km_ref[...], nt, preferred_element_type=F32) \
        + jnp.concatenate([bmeta, bmeta], axis=0)
    m0 = jnp.max(s, axis=-1, keepdims=True)
    m_sc[...] = jnp.broadcast_to(m0, (2 * tq, LANES))
    acc_sc[...] = jnp.dot(jnp.exp2((s - m0).astype(BF16)), vm, preferred_element_type=F32)

    def body(c, carry):
        off = pl.multiple_of(c * tk, tk)
        vc = jnp.concatenate([v_ref[pl.ds(off, tk), :], ones_col], axis=1)
        s = s0_sc[...]
        for n, (row0, qc, j) in enumerate(chains):
            if n + 1 < nch:
                s_ahead = scores(chains[n + 1][1], c)
            else:
                s_ahead = scores(chains[0][1], jnp.minimum(c + 1, nkc - 1))
            rows = pl.ds(row0, rq)
            bi = jnp.clip(c * (tk // rq) - (qi * nsub + j), d_lo, d_hi) - d_lo
            s = s + bt_ref[bi]
            m_prev = m_sc[rows, :]
            m_next = jnp.maximum(m_prev, jnp.max(s, axis=-1, keepdims=True))
            alpha = jnp.exp2(m_prev - m_next)
            p = jnp.exp2((s - jnp.tile(m_next, (1, tk // LANES))).astype(BF16))
            acc_sc[rows, :] = jnp.tile(alpha, (1, 2)) * acc_sc[rows, :] \
                + jnp.dot(p, vc, preferred_element_type=F32)
            m_sc[rows, :] = m_next
            s = s_ahead
        s0_sc[...] = s
        return carry

    lax.fori_loop(0, nkc, body, 0)
    acc = acc_sc[...]
    o = acc[:, :V_DIM] / acc[:, V_DIM:]
    o = o[:tq] - lam_ref[0] * o[tq:]
    ms = jnp.mean(o * o, axis=-1, keepdims=True)
    o = (o * lax.rsqrt(ms + SUBLN_EPS)) * g_ref[...] * (1.0 - LAM_INIT)
    o_ref[...] = o.astype(BF16)


def _bias_range(tq, tk):
    d_lo = -((tk - 1 + FAR_REL + tq - 1) // tq)
    d_hi = (FAR_REL + tq - 1 + tq - 1) // tq
    return d_lo, d_hi


def _relative_bucket(rel):
    half = NUM_BUCKETS // 2
    max_exact = half // 2
    n = jnp.abs(rel)
    large = max_exact + (jnp.log(jnp.maximum(n, 1).astype(F32) / max_exact)
                         / math.log(MAX_DISTANCE / max_exact) * (half - max_exact)).astype(jnp.int32)
    large = jnp.minimum(large, half - 1)
    return jnp.where(rel > 0, half, 0) + jnp.where(n < max_exact, n, large)


def _bias_tables(rel_bias, S, rq, tk):
    d_lo, d_hi = _bias_range(rq, tk)
    rb = rel_bias.astype(F32) * LOG2E

    def lookup(rel):
        onehot = (_relative_bucket(rel)[..., None]
                  == jnp.arange(NUM_BUCKETS, dtype=jnp.int32)).astype(F32)
        return jnp.einsum('xyzb,bh->hxyz', onehot, rb, precision=lax.Precision.HIGHEST)

    r = jnp.arange(rq, dtype=jnp.int32)[None, :, None]
    j = jnp.arange(tk, dtype=jnp.int32)[None, None, :]
    d = jnp.arange(d_lo, d_hi + 1, dtype=jnp.int32)[:, None, None]
    tab = lookup(j - r + d * rq)
    jm = jnp.arange(LANES, dtype=jnp.int32)[None, None, :]
    t = jnp.arange(S, dtype=jnp.int32)[None, :, None]
    tabm = jnp.where(jm < N_META, lookup(jm - (N_META + t)), NEG_BIG)[:, 0]
    return tab, tabm, d_lo, d_hi


def _attention(lam, q, k, v, km, vm, rel_bias, g_subln, B, S, tq, rq, tk):
    T = B * S
    nq = S // tq
    tab, tabm, d_lo, d_hi = _bias_tables(rel_bias, S, rq, tk)
    nd = d_hi - d_lo + 1
    kern = functools.partial(_attn_kernel, tq=tq, rq=rq, tk=tk, nkc=S // tk, d_lo=d_lo, d_hi=d_hi)
    return pl.pallas_call(
        kern,
        grid=(B, N_HEADS, nq),
        in_specs=[
            pl.BlockSpec(memory_space=pltpu.SMEM),
            pl.BlockSpec((tq, V_DIM), lambda b, h, i: (b * nq + i, h)),
            pl.BlockSpec((S, V_DIM), lambda b, h, i: (b, h)),
            pl.BlockSpec((S, V_DIM), lambda b, h, i: (b, h)),
            pl.BlockSpec((LANES, V_DIM), lambda b, h, i: (0, h)),
            pl.BlockSpec((LANES, V_DIM), lambda b, h, i: (0, h)),
            pl.BlockSpec((None, nd, rq, tk), lambda b, h, i: (h, 0, 0, 0)),
            pl.BlockSpec((None, tq, LANES), lambda b, h, i: (h, i, 0)),
            pl.BlockSpec((1, V_DIM), lambda b, h, i: (0, 0)),
        ],
        out_specs=pl.BlockSpec((tq, V_DIM), lambda b, h, i: (b * nq + i, h)),
        out_shape=jax.ShapeDtypeStruct((T, ATTN_WIDTH), BF16),
        scratch_shapes=[pltpu.VMEM((2 * tq, LANES), F32),
                        pltpu.VMEM((2 * tq, 2 * V_DIM), F32),
                        pltpu.VMEM((rq, tk), F32)],
        compiler_params=_cparams(("arbitrary", "arbitrary", "arbitrary")),
        name="attn",
    )(lam, q, k, v, km, vm, tab, tabm, g_subln)


def _seq_dft_kernel(c_ref, s_ref, cm_ref, ym_ref, yc_ref, ys_ref, o_ref, acc_sc, *, tn):
    nt = pl.program_id(2)

    @pl.when(nt == 0)
    def _():
        acc_sc[...] = jnp.dot(cm_ref[...], ym_ref[...], preferred_element_type=F32)

    off = pl.multiple_of(nt * tn, tn)
    acc_sc[...] += (jnp.dot(c_ref[...], yc_ref[pl.ds(off, tn), :], preferred_element_type=F32)
                    + jnp.dot(s_ref[...], ys_ref[pl.ds(off, tn), :], preferred_element_type=F32))

    @pl.when(nt == pl.num_programs(2) - 1)
    def _():
        o_ref[...] = acc_sc[...].astype(BF16)


def _dft_matrices(S):
    L = S + N_META
    scale = 1.0 / math.sqrt(L * GROUP_DIM)
    pos = N_META + jnp.arange(S, dtype=jnp.int32)
    w = 2.0 * math.pi / L
    na = S // LANES
    col_a = N_META + LANES * jnp.arange(na, dtype=jnp.int32)
    col_j = jnp.arange(LANES, dtype=jnp.int32)
    ang_a = ((pos[:, None] * col_a[None, :]) % L).astype(F32) * w
    ang_j = ((pos[:, None] * col_j[None, :]) % L).astype(F32) * w
    ca, sa = jnp.cos(ang_a)[:, :, None], jnp.sin(ang_a)[:, :, None]
    cj, sj = jnp.cos(ang_j)[:, None, :], jnp.sin(ang_j)[:, None, :]
    cmat = ((ca * cj - sa * sj) * scale).astype(BF16).reshape(S, S)
    smat = ((sa * cj + ca * sj) * (-scale)).astype(BF16).reshape(S, S)
    angm = ((pos[:, None] * jnp.arange(N_META, dtype=jnp.int32)[None, :]) % L).astype(F32) \
        * (2.0 * math.pi / L)
    cm = jnp.concatenate([jnp.cos(angm) * scale, jnp.sin(angm) * (-scale)], axis=1).astype(BF16)
    return cmat, smat, cm


def _seq_dft(yc, ys, ymeta, B, S, tmo, tn):
    T = B * S
    nk = S // tmo
    cmat, smat, cm = _dft_matrices(S)
    return pl.pallas_call(
        functools.partial(_seq_dft_kernel, tn=tn),
        grid=(B, nk, S // tn),
        in_specs=[
            pl.BlockSpec((tmo, tn), lambda b, kt, nt: (kt, nt)),
            pl.BlockSpec((tmo, tn), lambda b, kt, nt: (kt, nt)),
            pl.BlockSpec((tmo, 2 * N_META), lambda b, kt, nt: (kt, 0)),
            pl.BlockSpec((2 * N_META, FOURIER_WIDTH), lambda b, kt, nt: (0, 0)),
            pl.BlockSpec((S, FOURIER_WIDTH), lambda b, kt, nt: (b, 0)),
            pl.BlockSpec((S, FOURIER_WIDTH), lambda b, kt, nt: (b, 0)),
        ],
        out_specs=pl.BlockSpec((tmo, FOURIER_WIDTH), lambda b, kt, nt: (b * nk + kt, 0)),
        out_shape=jax.ShapeDtypeStruct((T, FOURIER_WIDTH), BF16),
        scratch_shapes=[pltpu.VMEM((tmo, FOURIER_WIDTH), F32)],
        compiler_params=_cparams(("arbitrary", "arbitrary", "arbitrary")),
        name="seq_dft",
    )(cmat, smat, cm, ymeta, yc, ys)


def _merge_kernel(*refs, n_x, n_first):
    (attn_ref, four_ref, sga_ref, sgf_ref), x_refs, (wab_ref, wfb_ref, wout_ref, gffn_ref, wrh_ref,
                                                    wrl_ref, br_ref, h_ref, n2_ref, gate_ref,
                                                    idx_ref) = refs[:4], refs[4:4 + n_x], refs[4 + n_x:]
    a = jnp.dot(attn_ref[...], wab_ref[...], preferred_element_type=F32)
    fb = jnp.dot(four_ref[...], wfb_ref[...], preferred_element_type=F32)
    mixed = sga_ref[...].astype(F32) * a + sgf_ref[...].astype(F32) * fb
    h = _pick_rows(x_refs, n_first) \
        + jnp.dot(mixed.astype(BF16), wout_ref[...], preferred_element_type=F32)
    h_ref[...] = h
    ms = jnp.mean(h * h, axis=-1, keepdims=True)
    n2 = (h * lax.rsqrt(ms + NORM_EPS)) * gffn_ref[...]
    n2_ref[...] = n2
    nh = n2.astype(BF16)
    nl = (n2 - nh.astype(F32)).astype(BF16)
    lg = (jnp.dot(nh, wrh_ref[...], preferred_element_type=F32)
          + jnp.dot(nl, wrh_ref[...], preferred_element_type=F32)
          + jnp.dot(nh, wrl_ref[...], preferred_element_type=F32)) + br_ref[...]
    col = lax.broadcasted_iota(jnp.int32, lg.shape, 1)
    colf = col.astype(F32)
    vals = []
    idx_out = jnp.zeros(lg.shape, jnp.int32)
    for kk in range(TOP_K):
        mx = jnp.max(lg, axis=-1, keepdims=True)
        am = jnp.min(jnp.where(lg == mx, colf, float(LANES)), axis=-1, keepdims=True)
        am = am.astype(jnp.int32)
        vals.append(mx)
        idx_out = jnp.where(col == kk, am, idx_out)
        lg = jnp.where(col == am, -jnp.inf, lg)
    es = [jnp.exp(v - vals[0]) for v in vals]
    den = es[0] + es[1] + es[2] + es[3]
    gate_out = jnp.zeros(lg.shape, F32)
    for kk in range(TOP_K):
        gate_out = jnp.where(col == kk, es[kk] / den, gate_out)
    gate_ref[...] = gate_out
    idx_ref[...] = idx_out


def _merge(attn, four, sga, sgf, xs, wab, wfb, wout, gffn, wrh, wrl, br, tm):
    T = attn.shape[0]
    row = lambda w: pl.BlockSpec((tm, w), lambda i: (i, 0))
    const = lambda a: pl.BlockSpec(a.shape, lambda i: (0,) * a.ndim)
    x_specs, n_first = _row_specs(xs, tm, D_MODEL)
    return pl.pallas_call(
        functools.partial(_merge_kernel, n_x=len(xs), n_first=n_first),
        grid=(T // tm,),
        in_specs=[row(ATTN_WIDTH), row(FOURIER_WIDTH), row(D_MODEL), row(D_MODEL)] + x_specs
        + [const(wab), const(wfb), const(wout), const(gffn), const(wrh), const(wrl), const(br)],
        out_specs=[row(D_MODEL), row(D_MODEL), row(LANES), row(LANES)],
        out_shape=[jax.ShapeDtypeStruct((T, D_MODEL), F32),
                   jax.ShapeDtypeStruct((T, D_MODEL), F32),
                   jax.ShapeDtypeStruct((T, LANES), F32),
                   jax.ShapeDtypeStruct((T, LANES), jnp.int32)],
        compiler_params=_cparams(("arbitrary",)),
        name="merge",
    )(attn, four, sga, sgf, *xs, wab, wfb, wout, gffn, wrh, wrl, br)


def _row_copy(src_ref, src_row, dst_ref, dst_row, sem):
    return pltpu.make_async_copy(src_ref.at[pl.ds(src_row, 1), :],
                                 dst_ref.at[pl.ds(dst_row, 1), :], sem)


DUMP_RING = 4


NBUF = 3


def _moe_kernel(be_ref, idx_hbm, x_hbm, wgu_ref, bgu_ref, wd_ref, bd_ref, y_hbm,
                idx_s, xb0, xb1, xb2, yb0, yb1, yb2, sem, *, bm, nb):
    del be_ref
    i = pl.program_id(0)
    xbs = (xb0, xb1, xb2)
    ybs = (yb0, yb1, yb2)

    def fetch_idx(row, slot):
        return pltpu.make_async_copy(idx_hbm.at[row], idx_s.at[slot], sem.at[2 * NBUF])

    def gather_all(b):
        return pltpu.make_async_copy(x_hbm.at[pl.ds(0, bm), :], xbs[b], sem.at[b])

    def scatter_all(b):
        return pltpu.make_async_copy(ybs[b], y_hbm.at[pl.ds(0, bm), :], sem.at[NBUF + b])

    @pl.when(i == 0)
    def _():
        for b in range(2):
            c0 = fetch_idx(b, b)
            c0.start()
            c0.wait()

            def g0(r, c, b=b):
                _row_copy(x_hbm, idx_s[b, 0, r], xbs[b], r, sem.at[b]).start()
                return c

            lax.fori_loop(0, bm, g0, 0, unroll=8)
        fetch_idx(2, 2).start()
        yb2[...] = jnp.zeros_like(yb2)

    def step(p):
        nxt = (p + 2) % NBUF
        fetch_idx(0, nxt).wait()
        fetch_idx(i + 3, p).start()
        gather_all(p).wait()

        @pl.when(i >= 2)
        def _():
            scatter_all(p).wait()

        def issue(n):
            r = n // 2
            if n % 2 == 0:
                _row_copy(x_hbm, idx_s[nxt, 0, r], xbs[nxt], r, sem.at[nxt]).start()
            else:
                _row_copy(ybs[nxt], r, y_hbm, idx_s[nxt, 1, r], sem.at[NBUF + nxt]).start()

        n_up, n_down = D_FF // LANES, D_MODEL // (2 * LANES)
        per = -(-2 * bm // (n_up + n_down))

        def issue_some(group):
            for n in range(group * per, min((group + 1) * per, 2 * bm)):
                issue(n)

        x = xbs[p][...].astype(BF16)
        acts = []
        for t in range(n_up):
            lo = 2 * LANES * t
            h = jnp.dot(x, wgu_ref[:, lo:lo + 2 * LANES], preferred_element_type=F32) \
                + bgu_ref[:, lo:lo + 2 * LANES]
            g = jnp.minimum(h[:, :LANES], SWIGLU_LIMIT)
            u = jnp.clip(h[:, LANES:], -SWIGLU_LIMIT, SWIGLU_LIMIT)
            acts.append(((u + 1.0) * (g * jax.nn.sigmoid(g * SWIGLU_ALPHA))).astype(BF16))
            issue_some(t)
        act = jnp.concatenate(acts, axis=1)
        for t in range(n_down):
            cols = slice(2 * LANES * t, 2 * LANES * (t + 1))
            ybs[p][:, cols] = jnp.dot(act, wd_ref[:, cols], preferred_element_type=F32) \
                + bd_ref[:, cols]
            issue_some(n_up + t)

        @pl.when(i == nb - 1)
        def _():
            gather_all((p + 1) % NBUF).wait()
            gather_all(nxt).wait()
            fetch_idx(0, p).wait()

            def s1(r, c):
                _row_copy(ybs[p], r, y_hbm, idx_s[p, 1, r], sem.at[NBUF + p]).start()
                return c

            lax.fori_loop(0, bm, s1, 0, unroll=8)
            for b in range(NBUF):
                scatter_all(b).wait()

    for p in range(NBUF):
        pl.when(i % NBUF == p)(functools.partial(step, p))


def _moe(block_e, idx_rows, n2, wgu, bgu, wd, bd, n_rows_out, bm):
    nb = block_e.shape[0]
    anyspec = pl.BlockSpec(memory_space=pl.ANY)
    xy = pltpu.VMEM((bm, D_MODEL), F32)
    return pl.pallas_call(
        functools.partial(_moe_kernel, bm=bm, nb=nb),
        grid_spec=pltpu.PrefetchScalarGridSpec(
            num_scalar_prefetch=1,
            grid=(nb,),
            in_specs=[anyspec, anyspec,
                      pl.BlockSpec((None, D_MODEL, 2 * D_FF), lambda i, be: (be[i], 0, 0)),
                      pl.BlockSpec((None, 1, 2 * D_FF), lambda i, be: (be[i], 0, 0)),
                      pl.BlockSpec((None, D_FF, D_MODEL), lambda i, be: (be[i], 0, 0)),
                      pl.BlockSpec((None, 1, D_MODEL), lambda i, be: (be[i], 0, 0))],
            out_specs=anyspec,
            scratch_shapes=[pltpu.SMEM((NBUF, 2, bm), jnp.int32)] + [xy] * (2 * NBUF)
            + [pltpu.SemaphoreType.DMA((2 * NBUF + 1,))]),
        out_shape=jax.ShapeDtypeStruct((n_rows_out, D_MODEL), F32),
        compiler_params=_cparams(("arbitrary",)),
        name="moe",
    )(block_e, idx_rows, n2, wgu, bgu, wd, bd)


def _routing(top_idx, T, bm):
    A = T * TOP_K
    flat_e = top_idx.reshape(-1)
    order = jnp.argsort(flat_e, stable=True).astype(jnp.int32)
    counts = jnp.sum((flat_e[:, None] == jnp.arange(N_EXPERTS, dtype=jnp.int32)[None, :])
                     .astype(jnp.int32), axis=0)
    nblk_e = (counts + bm - 1) // bm
    blk_end = jnp.cumsum(nblk_e)
    blk_start = blk_end - nblk_e
    grp_start = jnp.cumsum(counts) - counts
    nb = -(-A // bm) + N_EXPERTS
    blk = jnp.arange(nb, dtype=jnp.int32)
    block_e = jnp.minimum(jnp.sum((blk[:, None] >= blk_end[None, :]).astype(jnp.int32), axis=1),
                          N_EXPERTS - 1).astype(jnp.int32)
    lane = jnp.arange(bm, dtype=jnp.int32)[None, :]
    offs = (blk - blk_start[block_e])[:, None] * bm + lane
    valid = offs < counts[block_e][:, None]
    a = order[jnp.clip(grp_start[block_e][:, None] + offs, 0, A - 1)]
    row_src = jnp.where(valid, a // TOP_K, 0).astype(jnp.int32)
    dump = A + (blk % DUMP_RING)[:, None] * bm + lane
    row_dst = jnp.where(valid, (a % TOP_K) * T + a // TOP_K, dump).astype(jnp.int32)
    pre = jnp.broadcast_to(A + (DUMP_RING - 1) * bm + lane, (NBUF, bm)).astype(jnp.int32)
    src_rows = jnp.concatenate([row_src, jnp.zeros((NBUF, bm), jnp.int32)], axis=0)
    dst_rows = jnp.concatenate([pre, row_dst], axis=0)
    return block_e, jnp.stack([src_rows, dst_rows], axis=1)


def _final_kernel(h_ref, y0_ref, y1_ref, y2_ref, y3_ref, gate_ref, g_ref, o_ref):
    acc = h_ref[...]
    gate = gate_ref[...]
    for kk, y_ref in enumerate((y0_ref, y1_ref, y2_ref, y3_ref)):
        acc = acc + gate[:, kk:kk + 1] * y_ref[...]
    ms = jnp.mean(acc * acc, axis=-1, keepdims=True)
    o_ref[...] = (acc * lax.rsqrt(ms + NORM_EPS)) * g_ref[...]


def _final(h, y, gate, g_final, tm, row0, rows):
    T = h.shape[0]
    nt, t0 = T // tm, row0 // tm
    yspec = lambda kk: pl.BlockSpec((tm, D_MODEL), lambda i: (kk * nt + t0 + i, 0))
    return pl.pallas_call(
        _final_kernel,
        grid=(rows // tm,),
        in_specs=[pl.BlockSpec((tm, D_MODEL), lambda i: (t0 + i, 0)),
                  yspec(0), yspec(1), yspec(2), yspec(3),
                  pl.BlockSpec((tm, LANES), lambda i: (t0 + i, 0)),
                  pl.BlockSpec((1, D_MODEL), lambda i: (0, 0))],
        out_specs=pl.BlockSpec((tm, D_MODEL), lambda i: (i, 0)),
        out_shape=jax.ShapeDtypeStruct((rows, D_MODEL), F32),
        compiler_params=_cparams(("arbitrary",)),
        name="final",
    )(h, y, y, y, y, gate, g_final)


def _prep_gate_up_kernel(w_ref, p_ref, o_ref):
    for t in range(D_FF // LANES):
        lo = 2 * LANES * t
        o_ref[:, lo:lo + 2 * LANES] = jnp.dot(w_ref[:, lo:lo + 2 * LANES].astype(BF16), p_ref[...],
                                              preferred_element_type=F32).astype(BF16)


def _prep_gate_up(w_gate_up):
    E = w_gate_up.shape[0]
    j = jnp.arange(2 * LANES, dtype=jnp.int32)
    src_col = jnp.where(j < LANES, 2 * j, 2 * (j - LANES) + 1)
    perm = (jnp.arange(2 * LANES, dtype=jnp.int32)[:, None] == src_col[None, :]).astype(BF16)
    return pl.pallas_call(
        _prep_gate_up_kernel,
        grid=(E,),
        in_specs=[pl.BlockSpec((None, D_MODEL, 2 * D_FF), lambda e: (e, 0, 0)),
                  pl.BlockSpec((2 * LANES, 2 * LANES), lambda e: (0, 0))],
        out_specs=pl.BlockSpec((None, D_MODEL, 2 * D_FF), lambda e: (e, 0, 0)),
        out_shape=jax.ShapeDtypeStruct((E, D_MODEL, 2 * D_FF), BF16),
        compiler_params=_cparams(("arbitrary",)),
        name="prep_gate_up",
    )(w_gate_up, perm)


def _encode_tokens(xs, meta_tokens, g_mix, w_in, lambda_q1, lambda_k1, lambda_q2, lambda_k2,
                   g_subln, rel_bias, w_attn_branch, w_fourier_branch, w_out, g_ffn, w_router,
                   b_router, w_gate_up, b_gate_up, w_down, b_down, g_final):
    S, D = xs[0].shape[1:]
    B = sum(x.shape[0] for x in xs)
    T = B * S
    tm = min(512, S)
    tq, rq, tk = (min(t, S) for t in ATTN_TILES)
    tdft = min(1024, S)
    bm = 256

    xs2d = [x.reshape(-1, D) for x in xs]
    gm = g_mix[0].reshape(1, D)
    w_in_b = w_in[0].astype(BF16)
    cc = jnp.arange(GROUP_DIM, dtype=jnp.int32)
    angc = ((cc[:, None] * cc[None, :]) % GROUP_DIM).astype(F32) * (2.0 * math.pi / GROUP_DIM)
    dft_cs = jnp.concatenate([jnp.cos(angc), jnp.sin(angc)], axis=1).astype(BF16)

    q, k, v, yc, ys, sga, sgf = _in_proj(xs2d, gm, w_in_b, dft_cs, tm)
    _, k_m, v_m, yc_m, ys_m, _, _ = _in_proj([meta_tokens.astype(F32)], gm, w_in_b, dft_cs, N_META)

    lam = (jnp.exp(jnp.sum(lambda_q1[0].astype(F32) * lambda_k1[0].astype(F32)))
           - jnp.exp(jnp.sum(lambda_q2[0].astype(F32) * lambda_k2[0].astype(F32))) + LAM_INIT)
    pad = ((0, LANES - N_META), (0, 0))
    attn = _attention(lam.reshape(1), q, k, v, jnp.pad(k_m, pad), jnp.pad(v_m, pad), rel_bias,
                      g_subln[0].reshape(1, V_DIM), B, S, tq, rq, tk)

    ymeta = jnp.concatenate([yc_m, ys_m], axis=0)
    four = _seq_dft(yc, ys, ymeta, B, S, tdft, tdft)

    wr = jnp.pad(w_router[0], ((0, 0), (0, LANES - N_EXPERTS)))
    wrh = wr.astype(BF16)
    wrl = (wr - wrh.astype(F32)).astype(BF16)
    br = jnp.pad(b_router[0], (0, LANES - N_EXPERTS), constant_values=NEG_BIG).reshape(1, LANES)
    h, n2, gate, idx = _merge(attn, four, sga, sgf, xs2d, w_attn_branch[0].astype(BF16),
                              w_fourier_branch[0].astype(BF16), w_out[0].astype(BF16),
                              g_ffn[0].reshape(1, D), wrh, wrl, br, tm)

    block_e, idx_rows = _routing(idx[:, :TOP_K], T, bm)
    wgu = _prep_gate_up(w_gate_up[0])
    ng = D_FF // LANES
    bgu = b_gate_up[0].reshape(N_EXPERTS, ng, LANES, 2).transpose(0, 1, 3, 2) \
        .reshape(N_EXPERTS, 1, 2 * D_FF)
    y = _moe(block_e, idx_rows, n2, wgu, bgu, w_down[0].astype(BF16), b_down[0][:, None, :],
             TOP_K * T + DUMP_RING * bm, bm)

    outs, row0 = [], 0
    for x, x2d in zip(xs, xs2d):
        rows = x2d.shape[0]
        out = _final(h, y, gate, g_final.reshape(1, D), min(256, rows), row0, rows)
        outs.append(out.reshape(x.shape))
        row0 += rows
    return outs


def kernel(x_prompt, x_sample, meta_tokens, g_mix, w_in, lambda_q1, lambda_k1, lambda_q2, lambda_k2, g_subln, rel_bias, w_attn_branch, w_fourier_branch, w_out, g_ffn, w_router, b_router, w_gate_up, b_gate_up, w_down, b_down, g_final):
    y_prompt, y_sample = _encode_tokens(
        [x_prompt, x_sample], meta_tokens, g_mix, w_in, lambda_q1, lambda_k1, lambda_q2, lambda_k2,
        g_subln, rel_bias, w_attn_branch, w_fourier_branch, w_out, g_ffn, w_router, b_router,
        w_gate_up, b_gate_up, w_down, b_down, g_final)
    return y_prompt, y_sample
```
